```python
import jax, jax.numpy as jnp
from jax import lax
import numpy as np

D_MODEL = 1024
BATCH = 8
SEQ = 2048
DEPTH = 4

FOX_HEAD_DIM = 64
FOX_WIDTH = D_MODEL // 2
N_FOX_HEADS = FOX_WIDTH // FOX_HEAD_DIM
CONV_WIDTH = D_MODEL // 2
CONV_KERNEL = 3
N_MEM_HEADS = 4
MEM_HEAD_DIM = 128
MEM_WIDTH = N_MEM_HEADS * MEM_HEAD_DIM
MEM_TOKENS = 256
D_FF = -(-(8 * D_MODEL) // (3 * 256)) * 256
Q_BLOCK = 128
EPS = 1e-6
IN_SPLITS = (FOX_WIDTH, FOX_WIDTH, FOX_WIDTH, N_FOX_HEADS, CONV_WIDTH, CONV_WIDTH, CONV_WIDTH, D_MODEL, D_MODEL)
IN_WIDTH = sum(IN_SPLITS)

kernel_name = "fox_shortconv_gated_hybrid"


def rmsnorm(x, g):
    xf = x.astype(jnp.float32)
    y = xf * lax.rsqrt(jnp.mean(xf * xf, axis=-1, keepdims=True) + EPS)
    return (y * g.astype(jnp.float32)).astype(x.dtype)


def split_cols(p, sizes):
    out, start = [], 0
    for s in sizes:
        out.append(p[..., start:start + s])
        start += s
    return out


def forgetting_attention(q, k, v, log_f):
    b, t, h, hd = q.shape
    q = q.transpose(0, 2, 1, 3)
    k = k.transpose(0, 2, 1, 3)
    v = v.transpose(0, 2, 1, 3)
    c = jnp.cumsum(log_f, axis=1).transpose(0, 2, 1)
    scale = hd ** -0.5
    outs = []
    for i in range(t // Q_BLOCK):
        qs, ke = i * Q_BLOCK, (i + 1) * Q_BLOCK
        s = jnp.einsum('bhqd,bhkd->bhqk', q[:, :, qs:ke], k[:, :, :ke],
                       preferred_element_type=jnp.float32) * scale
        s = s + (c[:, :, qs:ke, None] - c[:, :, None, :ke])
        mask = (qs + jnp.arange(Q_BLOCK))[:, None] >= jnp.arange(ke)[None, :]
        p = jax.nn.softmax(jnp.where(mask, s, -jnp.inf), axis=-1).astype(v.dtype)
        outs.append(jnp.einsum('bhqk,bhkd->bhqd', p, v[:, :, :ke]))
    o = jnp.concatenate(outs, axis=2)
    return o.transpose(0, 2, 1, 3).reshape(b, t, h * hd)


def causal_depthwise_conv(z, w):
    ch = z.shape[-1]
    return lax.conv_general_dilated(
        z, w.astype(z.dtype)[:, None, :], window_strides=(1,),
        padding=[(CONV_KERNEL - 1, 0)], dimension_numbers=('NWC', 'WIO', 'NWC'),
        feature_group_count=ch)


def hybrid_mixer(u, w_in, b_f, q_gain, k_gain, conv_w, w_up_a, w_up_b, w_o):
    b, t, _ = u.shape
    p = u @ w_in
    q, k, v, f_logit, z, gate_b_in, gate_c_in, g_a, g_b = split_cols(p, IN_SPLITS)
    q = rmsnorm(q.reshape(b, t, N_FOX_HEADS, FOX_HEAD_DIM), q_gain)
    k = rmsnorm(k.reshape(b, t, N_FOX_HEADS, FOX_HEAD_DIM), k_gain)
    v = v.reshape(b, t, N_FOX_HEADS, FOX_HEAD_DIM)
    log_f = jax.nn.log_sigmoid(f_logit.astype(jnp.float32) + b_f.astype(jnp.float32))
    a_out = forgetting_attention(q, k, v, log_f)
    c_out = gate_b_in * causal_depthwise_conv(gate_c_in * z, conv_w)
    merged = jax.nn.sigmoid(g_a) * (a_out @ w_up_a) + jax.nn.sigmoid(g_b) * (c_out @ w_up_b)
    return merged @ w_o


def memory_attention(u, mem_n, w_cq, w_ckv, cq_gain, ck_gain, w_co):
    b, t, _ = u.shape
    m = mem_n.shape[1]
    q = rmsnorm((u @ w_cq).reshape(b, t, N_MEM_HEADS, MEM_HEAD_DIM), cq_gain)
    kv = mem_n @ w_ckv
    k = rmsnorm(kv[..., :MEM_WIDTH].reshape(b, m, N_MEM_HEADS, MEM_HEAD_DIM), ck_gain)
    v = kv[..., MEM_WIDTH:].reshape(b, m, N_MEM_HEADS, MEM_HEAD_DIM)
    s = jnp.einsum('bthd,bmhd->bhtm', q, k, preferred_element_type=jnp.float32) * (MEM_HEAD_DIM ** -0.5)
    p = jax.nn.softmax(s, axis=-1).astype(v.dtype)
    o = jnp.einsum('bhtm,bmhd->bthd', p, v).reshape(b, t, MEM_WIDTH)
    return o @ w_co


def swiglu(u, w_gu, w_down):
    gu = u @ w_gu
    return (jax.nn.silu(gu[..., :D_FF]) * gu[..., D_FF:]) @ w_down


def setup_inputs(seed: int = 0) -> dict:
    key = jax.random.key(seed)
    ks = jax.random.split(key, 24)
    L, D = DEPTH, D_MODEL

    def w(k, shape, fan_in):
        return jax.random.normal(k, shape, jnp.float32) * (fan_in ** -0.5)

    def gain(k, shape):
        return 1.0 + 0.05 * jax.random.normal(k, shape, jnp.float32)

    return {
        "x": jax.random.normal(ks[0], (BATCH, SEQ, D), jnp.float32),
        "mem": jax.random.normal(ks[1], (BATCH, MEM_TOKENS, D), jnp.float32),
        "norm_mix": gain(ks[2], (L, D)),
        "w_in": w(ks[3], (L, D, IN_WIDTH), D),
        "b_f": 2.0 + 0.5 * jax.random.normal(ks[4], (L, N_FOX_HEADS), jnp.float32),
        "q_gain": gain(ks[5], (L, FOX_HEAD_DIM)),
        "k_gain": gain(ks[6], (L, FOX_HEAD_DIM)),
        "conv_w": w(ks[7], (L, CONV_KERNEL, CONV_WIDTH), CONV_KERNEL),
        "w_up_a": w(ks[8], (L, FOX_WIDTH, D), FOX_WIDTH),
        "w_up_b": w(ks[9], (L, CONV_WIDTH, D), CONV_WIDTH),
        "w_o": w(ks[10], (L, D, D), 2 * D),
        "norm_mem_q": gain(ks[11], (L, D)),
        "norm_mem_kv": gain(ks[12], (L, D)),
        "w_cq": w(ks[13], (L, D, MEM_WIDTH), D),
        "w_ckv": w(ks[14], (L, D, 2 * MEM_WIDTH), D),
        "cq_gain": gain(ks[15], (L, MEM_HEAD_DIM)),
        "ck_gain": gain(ks[16], (L, MEM_HEAD_DIM)),
        "w_co": w(ks[17], (L, MEM_WIDTH, D), MEM_WIDTH),
        "norm_ffn": gain(ks[18], (L, D)),
        "w_gu": w(ks[19], (L, D, 2 * D_FF), D),
        "w_down": w(ks[20], (L, D_FF, D), D_FF),
    }


def reference(x, mem, norm_mix, w_in, b_f, q_gain, k_gain, conv_w, w_up_a, w_up_b, w_o,
              norm_mem_q, norm_mem_kv, w_cq, w_ckv, cq_gain, ck_gain, w_co,
              norm_ffn, w_gu, w_down):
    h = x
    for l in range(DEPTH):
        h = h + hybrid_mixer(rmsnorm(h, norm_mix[l]), w_in[l], b_f[l], q_gain[l], k_gain[l],
                             conv_w[l], w_up_a[l], w_up_b[l], w_o[l])
        h = h + memory_attention(rmsnorm(h, norm_mem_q[l]), rmsnorm(mem, norm_mem_kv[l]),
                                 w_cq[l], w_ckv[l], cq_gain[l], ck_gain[l], w_co[l])
        h = h + swiglu(rmsnorm(h, norm_ffn[l]), w_gu[l], w_down[l])
    return h
```

```python
import functools

import jax
import jax.numpy as jnp
from jax import lax
from jax.experimental import pallas as pl
from jax.experimental.pallas import tpu as pltpu

_BF16 = jnp.bfloat16
_F32 = jnp.float32

EPS = 1e-6
LANES = 128
SUBLANES = 8
VMEM_LIMIT_BYTES = 56 * 1024 * 1024
MASK_VALUE = -1e30
TOKEN_TILE = 512
ATTN_Q_TILE = 512
ATTN_K_TILE = 512
CUMSUM_BLOCK = 256
FFN_CHUNK = 256


def _dot(a, b):
    return jnp.dot(a, b, preferred_element_type=_F32)


def _dot_nt(a, b):
    return lax.dot_general(a, b, (((1,), (1,)), ((), ())), preferred_element_type=_F32)


def _rms_rows(x, gain):
    ms = jnp.mean(x * x, axis=-1, keepdims=True)
    return x * lax.rsqrt(ms + EPS) * gain


def _split3(x):
    hi = x.astype(_BF16)
    r1 = x - hi.astype(_F32)
    mid = r1.astype(_BF16)
    lo = (r1 - mid.astype(_F32)).astype(_BF16)
    return hi, mid, lo


def _params(*sem):
    return pltpu.CompilerParams(dimension_semantics=sem, vmem_limit_bytes=VMEM_LIMIT_BYTES)


def _resident(block_shape, index_map):
    return pl.BlockSpec(block_shape, index_map, pipeline_mode=pl.Buffered(1))


def _memkv_kernel(mem_ref, g_ref, w_ref, ckg_ref, k_ref, v_ref, *, n_heads, head_dim):
    u = _rms_rows(mem_ref[...], g_ref[...]).astype(_BF16)
    kv = _dot(u, w_ref[...])
    width = n_heads * head_dim
    for e in range(n_heads):
        ke = kv[:, e * head_dim:(e + 1) * head_dim]
        ms = jnp.mean(ke * ke, axis=-1, keepdims=True)
        k_ref[:, e * head_dim:(e + 1) * head_dim] = (
            ke * lax.rsqrt(ms + EPS) * ckg_ref[...]).astype(_BF16)
    v_ref[...] = kv[:, width:].astype(_BF16)


def _memkv(mem, norm_mem_kv, w_ckv_bf, ck_gain):
    n_layers, d_model, two_w = w_ckv_bf.shape
    batch, m_tok, _ = mem.shape
    width = two_w // 2
    head_dim = ck_gain.shape[-1]
    n_heads = width // head_dim
    out = jax.ShapeDtypeStruct((n_layers, batch, m_tok, width), _BF16)
    return pl.pallas_call(
        functools.partial(_memkv_kernel, n_heads=n_heads, head_dim=head_dim),
        grid=(n_layers, batch),
        in_specs=[
            pl.BlockSpec((None, m_tok, d_model), lambda l, b: (b, 0, 0)),
            pl.BlockSpec((None, 1, d_model), lambda l, b: (l, 0, 0)),
            pl.BlockSpec((None, d_model, two_w), lambda l, b: (l, 0, 0)),
            pl.BlockSpec((None, 1, head_dim), lambda l, b: (l, 0, 0)),
        ],
        out_specs=[
            pl.BlockSpec((None, None, m_tok, width), lambda l, b: (l, b, 0, 0)),
            pl.BlockSpec((None, None, m_tok, width), lambda l, b: (l, b, 0, 0)),
        ],
        out_shape=[out, out],
        compiler_params=_params("arbitrary", "arbitrary"),
        name="memkv",
    )(mem, norm_mem_kv[:, None, :], w_ckv_bf, ck_gain[:, None, :])


def _inproj_kernel(h_ref, g_ref, w_ref, bf_ref, qg_ref, kg_ref, cw_ref, gsum_ref,
                   qkv_ref, lf_ref, cc_ref, gate_ref, zc_s,
                   *, tm, fox_w, conv_w, gate_w, head_dim):
    j = pl.program_id(1)
    u = _rms_rows(h_ref[...], g_ref[...]).astype(_BF16)

    for idx, gain_ref in ((0, qg_ref), (1, kg_ref)):
        a = _dot(u, w_ref[:, idx * fox_w:(idx + 1) * fox_w])
        ss = _dot((a * a).astype(_BF16), gsum_ref[...])
        qkv_ref[:, idx * fox_w:(idx + 1) * fox_w] = (
            a * lax.rsqrt(ss * (1.0 / head_dim) + EPS) * gain_ref[...]).astype(_BF16)
    qkv_ref[:, 2 * fox_w:3 * fox_w] = _dot(u, w_ref[:, 2 * fox_w:3 * fox_w]).astype(_BF16)

    o = 3 * fox_w
    z = _dot(u, w_ref[:, o:o + conv_w])
    gate_b = _dot(u, w_ref[:, o + conv_w:o + 2 * conv_w])
    gate_c = _dot(u, w_ref[:, o + 2 * conv_w:o + 3 * conv_w])
    zc = gate_c * z

    @pl.when(j == 0)
    def _():
        zc_s[0:SUBLANES, :] = jnp.zeros((SUBLANES, conv_w), _F32)

    zc_s[SUBLANES:SUBLANES + tm, :] = zc
    y = (cw_ref[2:3, :] * zc
         + cw_ref[1:2, :] * zc_s[SUBLANES - 1:SUBLANES - 1 + tm, :]
         + cw_ref[0:1, :] * zc_s[SUBLANES - 2:SUBLANES - 2 + tm, :])
    cc_ref[...] = (gate_b * y).astype(_BF16)
    zc_s[0:SUBLANES, :] = zc_s[tm:tm + SUBLANES, :]

    o += 3 * conv_w
    step = conv_w
    for i in range(gate_w // step):
        a = _dot(u, w_ref[:, o + i * step:o + (i + 1) * step])
        gate_ref[:, i * step:(i + 1) * step] = jax.nn.sigmoid(a).astype(_BF16)

    o += gate_w
    a = _dot(u, w_ref[:, o:o + LANES]) + bf_ref[...]
    lf_ref[...] = jnp.minimum(a, 0.0) - jnp.log1p(jnp.exp(-jnp.abs(a)))


def _inproj(h, layer, gain, w_in_p, bf_pad, qg, kg, conv_w_l, gsum, *, batch, seq, dims):
    fox_w, conv_wd, gate_w, head_dim = dims
    n_tok, d_model = h.shape
    tm = TOKEN_TILE
    nt = seq // tm
    in_w = w_in_p.shape[-1]
    row = lambda b, j: (b * nt + j, 0)
    const2 = lambda b, j: (0, 0)
    return pl.pallas_call(
        functools.partial(_inproj_kernel, tm=tm, fox_w=fox_w, conv_w=conv_wd, gate_w=gate_w,
                          head_dim=head_dim),
        grid=(batch, nt),
        in_specs=[
            pl.BlockSpec((tm, d_model), row),
            _resident((1, d_model), const2),
            _resident((None, d_model, in_w), lambda b, j: (layer, 0, 0)),
            _resident((1, LANES), const2),
            _resident((1, fox_w), const2),
            _resident((1, fox_w), const2),
            _resident(conv_w_l.shape, const2),
            _resident(gsum.shape, const2),
        ],
        out_specs=[
            pl.BlockSpec((tm, 3 * fox_w), row),
            pl.BlockSpec((tm, LANES), row),
            pl.BlockSpec((tm, conv_wd), row),
            pl.BlockSpec((tm, gate_w), row),
        ],
        out_shape=[
            jax.ShapeDtypeStruct((n_tok, 3 * fox_w), _BF16),
            jax.ShapeDtypeStruct((n_tok, LANES), _F32),
            jax.ShapeDtypeStruct((n_tok, conv_wd), _BF16),
            jax.ShapeDtypeStruct((n_tok, gate_w), _BF16),
        ],
        scratch_shapes=[pltpu.VMEM((tm + SUBLANES, conv_wd), _F32)],
        compiler_params=_params("arbitrary", "arbitrary"),
        name="inproj",
    )(h, gain, w_in_p, bf_pad, qg, kg, conv_w_l, gsum)


def _decay_kernel(lf_ref, tri_ref, sel_ref, ccol_ref, crow_ref, *, seq, blk, n_pairs):
    tri = tri_ref[...]
    sel = sel_ref[...]
    carry = jnp.zeros((1, LANES), _F32)
    carry_sel = jnp.zeros((1, n_pairs * LANES), _F32)
    for i in range(seq // blk):
        parts = _split3(lf_ref[i * blk:(i + 1) * blk, :])
        parts_sel = [_dot(p, sel).astype(_BF16) for p in parts]
        c = (_dot(tri, parts[2]) + _dot(tri, parts[1])) + _dot(tri, parts[0]) + carry
        cs = (_dot(tri, parts_sel[2]) + _dot(tri, parts_sel[1])) + _dot(tri, parts_sel[0]) + carry_sel
        ct = c.T
        for p in range(n_pairs):
            ccol_ref[p, i * blk:(i + 1) * blk, :] = cs[:, p * LANES:(p + 1) * LANES]
            crow_ref[p, :, i * blk:(i + 1) * blk] = ct[2 * p:2 * p + 2, :]
        carry = c[blk - 1:blk, :]
        carry_sel = cs[blk - 1:blk, :]


def _decay(logf, tri, sel, *, batch, seq, n_pairs):
    blk = tri.shape[0]
    return pl.pallas_call(
        functools.partial(_decay_kernel, seq=seq, blk=blk, n_pairs=n_pairs),
        grid=(batch,),
        in_specs=[
            pl.BlockSpec((seq, LANES), lambda b: (b, 0)),
            _resident(tri.shape, lambda b: (0, 0)),
            _resident(sel.shape, lambda b: (0, 0)),
        ],
        out_specs=[
            pl.BlockSpec((None, n_pairs, seq, LANES), lambda b: (b, 0, 0, 0)),
            pl.BlockSpec((None, n_pairs, 2, seq), lambda b: (b, 0, 0, 0)),
        ],
        out_shape=[
            jax.ShapeDtypeStruct((batch, n_pairs, seq, LANES), _F32),
            jax.ShapeDtypeStruct((batch, n_pairs, 2, seq), _F32),
        ],
        compiler_params=_params("arbitrary"),
        name="decay",
    )(logf, tri, sel)


def _fox_kernel(q_ref, k_ref, v_ref, cq_ref, ck_ref, o_ref, *, tq, tk, head_dim):
    qi = pl.program_id(2)
    q = q_ref[...]
    lane = lax.broadcasted_iota(jnp.int32, (1, 2 * head_dim), 1)
    first = lane < head_dim
    zero = jnp.zeros_like(q)
    q_heads = (jnp.where(first, q, zero), jnp.where(first, zero, q))
    cq_pair = cq_ref[...]
    cq = (cq_pair[:, 0:1], cq_pair[:, head_dim:head_dim + 1])

    def step(j, carry, masked):
        ks = pl.multiple_of(j * tk, tk)
        kc = k_ref[pl.ds(ks, tk), :]
        vc = v_ref[pl.ds(ks, tk), :]
        out = []
        for e in range(2):
            m, l, acc = carry[e]
            s = _dot_nt(q_heads[e], kc)
            s = s + (cq[e] - ck_ref[e:e + 1, pl.ds(ks, tk)])
            if masked:
                rows = lax.broadcasted_iota(jnp.int32, (tq, tk), 0)
                cols = lax.broadcasted_iota(jnp.int32, (tq, tk), 1)
                s = jnp.where(rows >= cols, s, MASK_VALUE)
            m_new = jnp.maximum(m, jnp.max(s, axis=1, keepdims=True))
            alpha = jnp.exp(m - m_new)
            p = jnp.exp(s - m_new)
            l = alpha * l + jnp.sum(p, axis=1, keepdims=True)
            acc = alpha * acc + _dot(p.astype(_BF16), vc)
            out.append((m_new, l, acc))
        return tuple(out)

    init = tuple((jnp.full((tq, 1), MASK_VALUE, _F32), jnp.zeros((tq, 1), _F32),
                  jnp.zeros((tq, 2 * head_dim), _F32)) for _ in range(2))
    carry = lax.fori_loop(0, qi, lambda j, c: step(j, c, False), init)
    (_, l0, acc0), (_, l1, acc1) = step(qi, carry, True)
    o_ref[...] = jnp.where(first, acc0 / l0, acc1 / l1).astype(_BF16)


def _fox(qkv, ccol, crow, *, batch, seq, n_heads, head_dim):
    n_tok = qkv.shape[0]
    fox_w = n_heads * head_dim
    n_pairs = n_heads // 2
    pair_w = 2 * head_dim
    tq, tk = ATTN_Q_TILE, ATTN_K_TILE
    assert tq == tk and pair_w == LANES
    nq = seq // tq
    return pl.pallas_call(
        functools.partial(_fox_kernel, tq=tq, tk=tk, head_dim=head_dim),
        grid=(batch, n_pairs, nq),
        in_specs=[
            pl.BlockSpec((tq, pair_w), lambda b, p, i: (b * nq + i, p)),
            pl.BlockSpec((seq, pair_w), lambda b, p, i: (b, n_pairs + p)),
            pl.BlockSpec((seq, pair_w), lambda b, p, i: (b, 2 * n_pairs + p)),
            pl.BlockSpec((None, None, tq, LANES), lambda b, p, i: (b, p, i, 0)),
            pl.BlockSpec((None, None, 2, seq), lambda b, p, i: (b, p, 0, 0)),
        ],
        out_specs=pl.BlockSpec((tq, pair_w), lambda b, p, i: (b * nq + i, p)),
        out_shape=jax.ShapeDtypeStruct((n_tok, fox_w), _BF16),
        compiler_params=_params("arbitrary", "arbitrary", "arbitrary"),
        name="fox",
    )(qkv, qkv, qkv, ccol, crow)


def _post_kernel(a_ref, cc_ref, gate_ref, h_ref, wua_ref, wub_ref, wo_ref, gq_ref, wcq_ref,
                 cqg_ref, km_ref, vm_ref, wco_ref, o_ref, *, n_heads, head_dim):
    d_model = h_ref.shape[-1]
    up_a = _dot(a_ref[...], wua_ref[...])
    up_b = _dot(cc_ref[...], wub_ref[...])
    merged = (gate_ref[:, :d_model].astype(_F32) * up_a
              + gate_ref[:, d_model:].astype(_F32) * up_b)
    h1 = h_ref[...] + _dot(merged.astype(_BF16), wo_ref[...])

    u = _rms_rows(h1, gq_ref[...]).astype(_BF16)
    qm = _dot(u, wcq_ref[...])
    scale = head_dim ** -0.5
    heads = []
    for e in range(n_heads):
        sl = slice(e * head_dim, (e + 1) * head_dim)
        qe = qm[:, sl]
        ms = jnp.mean(qe * qe, axis=-1, keepdims=True)
        qn = (qe * lax.rsqrt(ms + EPS) * cqg_ref[...]).astype(_BF16)
        s = _dot_nt(qn, km_ref[:, sl]) * scale
        p = jnp.exp(s - jnp.max(s, axis=1, keepdims=True))
        den = jnp.sum(p, axis=1, keepdims=True)
        heads.append((_dot(p.astype(_BF16), vm_ref[:, sl]) / den).astype(_BF16))
    o_ref[...] = h1 + _dot(jnp.concatenate(heads, axis=1), wco_ref[...])


def _post(a_out, cc, gates, h, layer, w_up_a, w_up_b, w_o, gq, w_cq, cq_gain_l, kmem, vmem, w_co,
          *, batch, seq):
    n_tok, d_model = h.shape
    tm = TOKEN_TILE
    nt = seq // tm
    fox_w = a_out.shape[1]
    conv_wd = cc.shape[1]
    m_tok, mem_w = kmem.shape[2], kmem.shape[3]
    head_dim = cq_gain_l.shape[-1]
    row = lambda b, j: (b * nt + j, 0)
    const2 = lambda b, j: (0, 0)
    lay3 = lambda b, j: (layer, 0, 0)
    return pl.pallas_call(
        functools.partial(_post_kernel, n_heads=mem_w // head_dim, head_dim=head_dim),
        grid=(batch, nt),
        in_specs=[
            pl.BlockSpec((tm, fox_w), row),
            pl.BlockSpec((tm, conv_wd), row),
            pl.BlockSpec((tm, 2 * d_model), row),
            pl.BlockSpec((tm, d_model), row),
            _resident((None, fox_w, d_model), lay3),
            _resident((None, conv_wd, d_model), lay3),
            _resident((None, d_model, d_model), lay3),
            _resident((1, d_model), const2),
            _resident((None, d_model, mem_w), lay3),
            _resident((1, head_dim), const2),
            pl.BlockSpec((None, None, m_tok, mem_w), lambda b, j: (layer, b, 0, 0)),
            pl.BlockSpec((None, None, m_tok, mem_w), lambda b, j: (layer, b, 0, 0)),
            _resident((None, mem_w, d_model), lay3),
        ],
        out_specs=pl.BlockSpec((tm, d_model), row),
        out_shape=jax.ShapeDtypeStruct((n_tok, d_model), _F32),
        compiler_params=_params("arbitrary", "arbitrary"),
        name="post",
    )(a_out, cc, gates, h, w_up_a, w_up_b, w_o, gq, w_cq, cq_gain_l, kmem, vmem, w_co)


def _ffn_kernel(h_ref, g_ref, wgu_ref, wd_ref, o_ref, act_s, *, d_ff, chunk):
    h = h_ref[...]
    u = _rms_rows(h, g_ref[...]).astype(_BF16)
    for c in range(d_ff // chunk):
        gate = _dot(u, wgu_ref[:, c * chunk:(c + 1) * chunk])
        up = _dot(u, wgu_ref[:, d_ff + c * chunk:d_ff + (c + 1) * chunk])
        act_s[:, c * chunk:(c + 1) * chunk] = (gate * jax.nn.sigmoid(gate) * up).astype(_BF16)
    o_ref[...] = h + _dot(act_s[...], wd_ref[...])


def _ffn(h, layer, gain, w_gu, w_down):
    n_tok, d_model = h.shape
    d_ff = w_down.shape[1]
    tm = TOKEN_TILE
    assert d_ff % FFN_CHUNK == 0
    return pl.pallas_call(
        functools.partial(_ffn_kernel, d_ff=d_ff, chunk=FFN_CHUNK),
        grid=(n_tok // tm,),
        in_specs=[
            pl.BlockSpec((tm, d_model), lambda i: (i, 0)),
            _resident((1, d_model), lambda i: (0, 0)),
            _resident((None, d_model, 2 * d_ff), lambda i: (layer, 0, 0)),
            _resident((None, d_ff, d_model), lambda i: (layer, 0, 0)),
        ],
        out_specs=pl.BlockSpec((tm, d_model), lambda i: (i, 0)),
        out_shape=jax.ShapeDtypeStruct((n_tok, d_model), _F32),
        scratch_shapes=[pltpu.VMEM((tm, d_ff), _BF16)],
        compiler_params=_params("arbitrary"),
        name="ffn",
    )(h, gain, w_gu, w_down)


def kernel(x, mem, norm_mix, w_in, b_f, q_gain, k_gain, conv_w, w_up_a, w_up_b, w_o,
           norm_mem_q, norm_mem_kv, w_cq, w_ckv, cq_gain, ck_gain, w_co, norm_ffn, w_gu, w_down):
    batch, seq, d_model = x.shape
    n_layers = w_in.shape[0]
    head_dim = q_gain.shape[-1]
    n_heads = b_f.shape[-1]
    fox_w = n_heads * head_dim
    conv_wd = conv_w.shape[-1]
    gate_w = 2 * d_model
    n_pairs = n_heads // 2
    assert seq % TOKEN_TILE == 0 and seq % ATTN_Q_TILE == 0 and seq % CUMSUM_BLOCK == 0
    assert w_in.shape[-1] == 3 * fox_w + n_heads + 3 * conv_wd + gate_w and n_heads <= LANES

    f0 = 3 * fox_w
    w_in_p = jnp.concatenate(
        [w_in[..., :f0], w_in[..., f0 + n_heads:], w_in[..., f0:f0 + n_heads],
         jnp.zeros((n_layers, d_model, LANES - n_heads), w_in.dtype)], axis=-1).astype(_BF16)
    bf_pad = jnp.pad(b_f, ((0, 0), (0, LANES - n_heads)))
    w_up_a_bf, w_up_b_bf, w_o_bf = (w.astype(_BF16) for w in (w_up_a, w_up_b, w_o))
    w_cq_bf, w_ckv_bf, w_co_bf = (w.astype(_BF16) for w in (w_cq, w_ckv, w_co))
    w_gu_bf, w_down_bf = w_gu.astype(_BF16), w_down.astype(_BF16)
    qg = jnp.tile(q_gain, (1, n_heads)) * (head_dim ** -0.5)
    kg = jnp.tile(k_gain, (1, n_heads))

    ids = jnp.arange(fox_w) // head_dim
    gsum = (ids[:, None] == ids[None, :]).astype(_BF16)
    r = jnp.arange(CUMSUM_BLOCK)
    tri = (r[:, None] >= r[None, :]).astype(_BF16)
    col = jnp.arange(n_pairs * LANES)
    src_head = 2 * (col // LANES) + (col % LANES) // head_dim
    sel = (jnp.arange(LANES)[:, None] == src_head[None, :]).astype(_BF16)

    kmem, vmem = _memkv(mem, norm_mem_kv, w_ckv_bf, ck_gain)

    h = x.reshape(batch * seq, d_model)
    dims = (fox_w, conv_wd, gate_w, head_dim)
    for l in range(n_layers):
        qkv, logf, cc, gates = _inproj(h, l, norm_mix[l][None], w_in_p, bf_pad[l][None],
                                       qg[l][None], kg[l][None], conv_w[l], gsum,
                                       batch=batch, seq=seq, dims=dims)
        ccol, crow = _decay(logf, tri, sel, batch=batch, seq=seq, n_pairs=n_pairs)
        a_out = _fox(qkv, ccol, crow, batch=batch, seq=seq, n_heads=n_heads, head_dim=head_dim)
        h = _post(a_out, cc, gates, h, l, w_up_a_bf, w_up_b_bf, w_o_bf, norm_mem_q[l][None],
                  w_cq_bf, cq_gain[l][None], kmem, vmem, w_co_bf, batch=batch, seq=seq)
        h = _ffn(h, l, norm_ffn[l][None], w_gu_bf, w_down_bf)
    return h.reshape(batch, seq, d_model)
```

```python
import functools

import jax
import jax.numpy as jnp
import numpy as np
from jax import lax
from jax.experimental import pallas as pl
from jax.experimental.pallas import tpu as pltpu

_BF16 = jnp.bfloat16
_F32 = jnp.float32

EPS = 1e-6
LOG2E = 1.4426950408889634
LANES = 128
SUBLANES = 8
VMEM_LIMIT_BYTES = 56 * 1024 * 1024
MASK_VALUE = -1e30
TOKEN_TILE = 512
ATTN_Q_TILE = 512
ATTN_K_TILE = 512
CUMSUM_BLOCK = 256
FFN_CHUNK = 256


def _dot(a, b):
    return jnp.dot(a, b, preferred_element_type=_F32)


def _dot_nt(a, b):
    return lax.dot_general(a, b, (((1,), (1,)), ((), ())), preferred_element_type=_F32)


def _rms_rows(x, gain):
    ms = jnp.mean(x * x, axis=-1, keepdims=True)
    return x * lax.rsqrt(ms + EPS) * gain


def _split3(x):
    hi = x.astype(_BF16)
    r1 = x - hi.astype(_F32)
    mid = r1.astype(_BF16)
    lo = (r1 - mid.astype(_F32)).astype(_BF16)
    return hi, mid, lo


def _params(*sem):
    return pltpu.CompilerParams(dimension_semantics=sem, vmem_limit_bytes=VMEM_LIMIT_BYTES)


def _resident(block_shape, index_map):
    return pl.BlockSpec(block_shape, index_map, pipeline_mode=pl.Buffered(1))


def _memkv_kernel(mem_ref, g_ref, w_ref, ckg_ref, k_ref, v_ref, *, n_heads, head_dim):
    u = _rms_rows(mem_ref[...], g_ref[...]).astype(_BF16)
    kv = _dot(u, w_ref[...])
    width = n_heads * head_dim
    for e in range(n_heads):
        ke = kv[:, e * head_dim:(e + 1) * head_dim]
        ms = jnp.mean(ke * ke, axis=-1, keepdims=True)
        k_ref[:, e * head_dim:(e + 1) * head_dim] = (
            ke * lax.rsqrt(ms + EPS) * ckg_ref[...]).astype(_BF16)
    v_ref[...] = kv[:, width:].astype(_BF16)


def _memkv(mem, norm_mem_kv, w_ckv_bf, ck_gain):
    n_layers, d_model, two_w = w_ckv_bf.shape
    batch, m_tok, _ = mem.shape
    width = two_w // 2
    head_dim = ck_gain.shape[-1]
    n_heads = width // head_dim
    out = jax.ShapeDtypeStruct((n_layers, batch, m_tok, width), _BF16)
    return pl.pallas_call(
        functools.partial(_memkv_kernel, n_heads=n_heads, head_dim=head_dim),
        grid=(n_layers, batch),
        in_specs=[
            pl.BlockSpec((None, m_tok, d_model), lambda l, b: (b, 0, 0)),
            pl.BlockSpec((None, 1, d_model), lambda l, b: (l, 0, 0)),
            pl.BlockSpec((None, d_model, two_w), lambda l, b: (l, 0, 0)),
            pl.BlockSpec((None, 1, head_dim), lambda l, b: (l, 0, 0)),
        ],
        out_specs=[
            pl.BlockSpec((None, None, m_tok, width), lambda l, b: (l, b, 0, 0)),
            pl.BlockSpec((None, None, m_tok, width), lambda l, b: (l, b, 0, 0)),
        ],
        out_shape=[out, out],
        compiler_params=_params("arbitrary", "arbitrary"),
        name="memkv",
    )(mem, norm_mem_kv[:, None, :], w_ckv_bf, ck_gain[:, None, :])


def _inproj_kernel(h_ref, g_ref, w_ref, bf_ref, qg_ref, kg_ref, cw_ref, gsum_ref,
                   qkv_ref, lf_ref, cc_ref, gate_ref, zc_s,
                   *, tm, fox_w, conv_w, gate_w, head_dim):
    j = pl.program_id(1)
    u = _rms_rows(h_ref[...], g_ref[...]).astype(_BF16)

    for idx, gain_ref in ((0, qg_ref), (1, kg_ref)):
        a = _dot(u, w_ref[:, idx * fox_w:(idx + 1) * fox_w])
        ss = _dot((a * a).astype(_BF16), gsum_ref[...])
        qkv_ref[:, idx * fox_w:(idx + 1) * fox_w] = (
            a * lax.rsqrt(ss * (1.0 / head_dim) + EPS) * gain_ref[...]).astype(_BF16)
    qkv_ref[:, 2 * fox_w:3 * fox_w] = _dot(u, w_ref[:, 2 * fox_w:3 * fox_w]).astype(_BF16)

    o = 3 * fox_w
    z = _dot(u, w_ref[:, o:o + conv_w])
    gate_b = _dot(u, w_ref[:, o + conv_w:o + 2 * conv_w])
    gate_c = _dot(u, w_ref[:, o + 2 * conv_w:o + 3 * conv_w])
    zc = gate_c * z

    @pl.when(j == 0)
    def _():
        zc_s[0:SUBLANES, :] = jnp.zeros((SUBLANES, conv_w), _F32)

    zc_s[SUBLANES:SUBLANES + tm, :] = zc
    y = (cw_ref[2:3, :] * zc
         + cw_ref[1:2, :] * zc_s[SUBLANES - 1:SUBLANES - 1 + tm, :]
         + cw_ref[0:1, :] * zc_s[SUBLANES - 2:SUBLANES - 2 + tm, :])
    cc_ref[...] = (gate_b * y).astype(_BF16)
    zc_s[0:SUBLANES, :] = zc_s[tm:tm + SUBLANES, :]

    o += 3 * conv_w
    step = conv_w
    for i in range(gate_w // step):
        a = _dot(u, w_ref[:, o + i * step:o + (i + 1) * step])
        gate_ref[:, i * step:(i + 1) * step] = jax.nn.sigmoid(a).astype(_BF16)

    o += gate_w
    a = _dot(u, w_ref[:, o:o + LANES]) + bf_ref[...]
    lf_ref[...] = jnp.minimum(a, 0.0) - jnp.log1p(jnp.exp(-jnp.abs(a)))


def _inproj(h, layer, gain, w_in_p, bf_pad, qg, kg, conv_w_l, gsum, *, batch, seq, dims):
    fox_w, conv_wd, gate_w, head_dim = dims
    n_tok, d_model = h.shape
    tm = TOKEN_TILE
    nt = seq // tm
    in_w = w_in_p.shape[-1]
    row = lambda b, j: (b * nt + j, 0)
    const2 = lambda b, j: (0, 0)
    return pl.pallas_call(
        functools.partial(_inproj_kernel, tm=tm, fox_w=fox_w, conv_w=conv_wd, gate_w=gate_w,
                          head_dim=head_dim),
        grid=(batch, nt),
        in_specs=[
            pl.BlockSpec((tm, d_model), row),
            _resident((1, d_model), const2),
            _resident((None, d_model, in_w), lambda b, j: (layer, 0, 0)),
            _resident((1, LANES), const2),
            _resident((1, fox_w), const2),
            _resident((1, fox_w), const2),
            _resident(conv_w_l.shape, const2),
            _resident(gsum.shape, const2),
        ],
        out_specs=[
            pl.BlockSpec((tm, 3 * fox_w), row),
            pl.BlockSpec((tm, LANES), row),
            pl.BlockSpec((tm, conv_wd), row),
            pl.BlockSpec((tm, gate_w), row),
        ],
        out_shape=[
            jax.ShapeDtypeStruct((n_tok, 3 * fox_w), _BF16),
            jax.ShapeDtypeStruct((n_tok, LANES), _F32),
            jax.ShapeDtypeStruct((n_tok, conv_wd), _BF16),
            jax.ShapeDtypeStruct((n_tok, gate_w), _BF16),
        ],
        scratch_shapes=[pltpu.VMEM((tm + SUBLANES, conv_wd), _F32)],
        compiler_params=_params("arbitrary", "arbitrary"),
        name="inproj",
    )(h, gain, w_in_p, bf_pad, qg, kg, conv_w_l, gsum)


def _decay_kernel(lf_ref, tri_ref, sq_ref, sk_ref, constq_ref, constk_ref, eq_ref, ek_ref,
                  *, seq, blk):
    tri = tri_ref[...]
    carry = jnp.zeros((1, LANES), _F32)
    for i in range(seq // blk):
        rows = slice(i * blk, (i + 1) * blk)
        hi, mid, lo = _split3(lf_ref[rows, :])
        c = (_dot(tri, lo) + _dot(tri, mid)) + _dot(tri, hi) + carry
        carry = c[blk - 1:blk, :]
        parts = jnp.concatenate(_split3(c * LOG2E), axis=1)
        eq_ref[rows, :] = (_dot(parts, sq_ref[...]) + constq_ref[...]).astype(_BF16)
        ek_ref[rows, :] = (_dot(parts, sk_ref[...]) + constk_ref[...]).astype(_BF16)


def _decay(logf, consts, *, batch, seq):
    tri, sq = consts[0], consts[1]
    blk = tri.shape[0]
    aug_w = sq.shape[1]
    n_tok = logf.shape[0]
    const = lambda b: (0, 0)
    out = jax.ShapeDtypeStruct((n_tok, aug_w), _BF16)
    return pl.pallas_call(
        functools.partial(_decay_kernel, seq=seq, blk=blk),
        grid=(batch,),
        in_specs=[pl.BlockSpec((seq, LANES), lambda b: (b, 0))]
        + [_resident(c.shape, const) for c in consts],
        out_specs=[pl.BlockSpec((seq, aug_w), lambda b: (b, 0))] * 2,
        out_shape=[out, out],
        compiler_params=_params("arbitrary"),
        name="decay",
    )(logf, *consts)


def _decay_constants(n_heads, head_dim):
    n_parts = 3
    r = np.arange(CUMSUM_BLOCK)
    tri = (r[:, None] >= r[None, :]).astype(np.float32)
    sq = np.zeros((n_parts * LANES, n_heads * LANES), np.float32)
    sk = np.zeros_like(sq)
    constq = np.zeros((1, n_heads * LANES), np.float32)
    constk = np.zeros_like(constq)
    for h in range(n_heads):
        base = h * LANES + (head_dim if h % 2 == 0 else 0)
        for part in range(n_parts):
            sq[part * LANES + h, base + part] = 1.0
            sk[part * LANES + h, base + n_parts + part] = -1.0
            constq[0, base + n_parts + part] = 1.0
            constk[0, base + part] = 1.0
    return (jnp.asarray(tri, _BF16), jnp.asarray(sq, _BF16), jnp.asarray(sk, _BF16),
            jnp.asarray(constq), jnp.asarray(constk))


def _fox_kernel(q_ref, eq0_ref, eq1_ref, k_ref, ek0_ref, ek1_ref, v_ref, o_ref,
                *, tq, tk, head_dim):
    qi = pl.program_id(2)
    q = q_ref[...]
    lane = lax.broadcasted_iota(jnp.int32, (1, 2 * head_dim), 1)
    first = lane < head_dim
    q_aug = (jnp.where(first, q, eq0_ref[...]), jnp.where(first, eq1_ref[...], q))

    def step(j, carry, masked):
        ks = pl.multiple_of(j * tk, tk)
        kc = k_ref[pl.ds(ks, tk), :]
        vc = v_ref[pl.ds(ks, tk), :]
        k_aug = (jnp.where(first, kc, ek0_ref[pl.ds(ks, tk), :]),
                 jnp.where(first, ek1_ref[pl.ds(ks, tk), :], kc))
        out = []
        for e in range(2):
            m, l, acc = carry[e]
            s = _dot_nt(q_aug[e], k_aug[e])
            if masked:
                rows = lax.broadcasted_iota(jnp.int32, (tq, tk), 0)
                cols = lax.broadcasted_iota(jnp.int32, (tq, tk), 1)
                s = jnp.where(rows >= cols, s, MASK_VALUE)
            m_new = jnp.maximum(m, jnp.max(s, axis=1, keepdims=True))
            alpha = jnp.exp2(m - m_new)
            p = jnp.exp2(s - m_new)
            psum = p[:, 0:LANES]
            for c in range(1, tk // LANES):
                psum = psum + p[:, c * LANES:(c + 1) * LANES]
            l = alpha * l + psum
            acc = alpha * acc + _dot(p.astype(_BF16), vc)
            out.append((m_new, l, acc))
        return tuple(out)

    init = tuple((jnp.full((tq, 1), MASK_VALUE, _F32), jnp.zeros((tq, LANES), _F32),
                  jnp.zeros((tq, 2 * head_dim), _F32)) for _ in range(2))
    carry = lax.fori_loop(0, qi, lambda j, c: step(j, c, False), init)
    (_, l0, acc0), (_, l1, acc1) = step(qi, carry, True)
    l0 = jnp.sum(l0, axis=1, keepdims=True)
    l1 = jnp.sum(l1, axis=1, keepdims=True)
    o_ref[...] = jnp.where(first, acc0 / l0, acc1 / l1).astype(_BF16)


def _fox(qkv, eq, ek, *, batch, seq, n_heads, head_dim):
    n_tok = qkv.shape[0]
    fox_w = n_heads * head_dim
    n_pairs = n_heads // 2
    pair_w = 2 * head_dim
    tq, tk = ATTN_Q_TILE, ATTN_K_TILE
    assert tq == tk and pair_w == LANES
    nq = seq // tq
    return pl.pallas_call(
        functools.partial(_fox_kernel, tq=tq, tk=tk, head_dim=head_dim),
        grid=(batch, n_pairs, nq),
        in_specs=[
            pl.BlockSpec((tq, pair_w), lambda b, p, i: (b * nq + i, p)),
            pl.BlockSpec((tq, LANES), lambda b, p, i: (b * nq + i, 2 * p)),
            pl.BlockSpec((tq, LANES), lambda b, p, i: (b * nq + i, 2 * p + 1)),
            pl.BlockSpec((seq, pair_w), lambda b, p, i: (b, n_pairs + p)),
            pl.BlockSpec((seq, LANES), lambda b, p, i: (b, 2 * p)),
            pl.BlockSpec((seq, LANES), lambda b, p, i: (b, 2 * p + 1)),
            pl.BlockSpec((seq, pair_w), lambda b, p, i: (b, 2 * n_pairs + p)),
        ],
        out_specs=pl.BlockSpec((tq, pair_w), lambda b, p, i: (b * nq + i, p)),
        out_shape=jax.ShapeDtypeStruct((n_tok, fox_w), _BF16),
        compiler_params=_params("arbitrary", "arbitrary", "arbitrary"),
        name="fox",
    )(qkv, eq, eq, qkv, ek, ek, qkv)


def _post_kernel(a_ref, cc_ref, gate_ref, h_ref, wua_ref, wub_ref, wo_ref, gq_ref, wcq_ref,
                 cqg_ref, km_ref, vm_ref, wco_ref, o_ref, *, n_heads, head_dim):
    d_model = h_ref.shape[-1]
    up_a = _dot(a_ref[...], wua_ref[...])
    up_b = _dot(cc_ref[...], wub_ref[...])
    merged = (gate_ref[:, :d_model].astype(_F32) * up_a
              + gate_ref[:, d_model:].astype(_F32) * up_b)
    h1 = h_ref[...] + _dot(merged.astype(_BF16), wo_ref[...])

    u = _rms_rows(h1, gq_ref[...]).astype(_BF16)
    qm = _dot(u, wcq_ref[...])
    scale = head_dim ** -0.5
    heads = []
    for e in range(n_heads):
        sl = slice(e * head_dim, (e + 1) * head_dim)
        qe = qm[:, sl]
        ms = jnp.mean(qe * qe, axis=-1, keepdims=True)
        qn = (qe * lax.rsqrt(ms + EPS) * cqg_ref[...]).astype(_BF16)
        s = _dot_nt(qn, km_ref[:, sl]) * scale
        p = jnp.exp(s - jnp.max(s, axis=1, keepdims=True))
        den = jnp.sum(p, axis=1, keepdims=True)
        heads.append((_dot(p.astype(_BF16), vm_ref[:, sl]) / den).astype(_BF16))
    o_ref[...] = h1 + _dot(jnp.concatenate(heads, axis=1), wco_ref[...])


def _post(a_out, cc, gates, h, layer, w_up_a, w_up_b, w_o, gq, w_cq, cq_gain_l, kmem, vmem, w_co,
          *, batch, seq):
    n_tok, d_model = h.shape
    tm = TOKEN_TILE
    nt = seq // tm
    fox_w = a_out.shape[1]
    conv_wd = cc.shape[1]
    m_tok, mem_w = kmem.shape[2], kmem.shape[3]
    head_dim = cq_gain_l.shape[-1]
    row = lambda b, j: (b * nt + j, 0)
    const2 = lambda b, j: (0, 0)
    lay3 = lambda b, j: (layer, 0, 0)
    return pl.pallas_call(
        functools.partial(_post_kernel, n_heads=mem_w // head_dim, head_dim=head_dim),
        grid=(batch, nt),
        in_specs=[
            pl.BlockSpec((tm, fox_w), row),
            pl.BlockSpec((tm, conv_wd), row),
            pl.BlockSpec((tm, 2 * d_model), row),
            pl.BlockSpec((tm, d_model), row),
            _resident((None, fox_w, d_model), lay3),
            _resident((None, conv_wd, d_model), lay3),
            _resident((None, d_model, d_model), lay3),
            _resident((1, d_model), const2),
            _resident((None, d_model, mem_w), lay3),
            _resident((1, head_dim), const2),
            pl.BlockSpec((None, None, m_tok, mem_w), lambda b, j: (layer, b, 0, 0)),
            pl.BlockSpec((None, None, m_tok, mem_w), lambda b, j: (layer, b, 0, 0)),
            _resident((None, mem_w, d_model), lay3),
        ],
        out_specs=pl.BlockSpec((tm, d_model), row),
        out_shape=jax.ShapeDtypeStruct((n_tok, d_model), _F32),
        compiler_params=_params("arbitrary", "arbitrary"),
        name="post",
    )(a_out, cc, gates, h, w_up_a, w_up_b, w_o, gq, w_cq, cq_gain_l, kmem, vmem, w_co)


def _ffn_kernel(h_ref, g_ref, wgu_ref, wd_ref, o_ref, act_s, *, d_ff, chunk):
    h = h_ref[...]
    u = _rms_rows(h, g_ref[...]).astype(_BF16)
    for c in range(d_ff // chunk):
        gate = _dot(u, wgu_ref[:, c * chunk:(c + 1) * chunk])
        up = _dot(u, wgu_ref[:, d_ff + c * chunk:d_ff + (c + 1) * chunk])
        act_s[:, c * chunk:(c + 1) * chunk] = (gate * jax.nn.sigmoid(gate) * up).astype(_BF16)
    o_ref[...] = h + _dot(act_s[...], wd_ref[...])


def _ffn(h, layer, gain, w_gu, w_down):
    n_tok, d_model = h.shape
    d_ff = w_down.shape[1]
    tm = TOKEN_TILE
    assert d_ff % FFN_CHUNK == 0
    return pl.pallas_call(
        functools.partial(_ffn_kernel, d_ff=d_ff, chunk=FFN_CHUNK),
        grid=(n_tok // tm,),
        in_specs=[
            pl.BlockSpec((tm, d_model), lambda i: (i, 0)),
            _resident((1, d_model), lambda i: (0, 0)),
            _resident((None, d_model, 2 * d_ff), lambda i: (layer, 0, 0)),
            _resident((None, d_ff, d_model), lambda i: (layer, 0, 0)),
        ],
        out_specs=pl.BlockSpec((tm, d_model), lambda i: (i, 0)),
        out_shape=jax.ShapeDtypeStruct((n_tok, d_model), _F32),
        scratch_shapes=[pltpu.VMEM((tm, d_ff), _BF16)],
        compiler_params=_params("arbitrary"),
        name="ffn",
    )(h, gain, w_gu, w_down)


def kernel(x, mem, norm_mix, w_in, b_f, q_gain, k_gain, conv_w, w_up_a, w_up_b, w_o,
           norm_mem_q, norm_mem_kv, w_cq, w_ckv, cq_gain, ck_gain, w_co, norm_ffn, w_gu, w_down):
    batch, seq, d_model = x.shape
    n_layers = w_in.shape[0]
    head_dim = q_gain.shape[-1]
    n_heads = b_f.shape[-1]
    fox_w = n_heads * head_dim
    conv_wd = conv_w.shape[-1]
    gate_w = 2 * d_model
    n_pairs = n_heads // 2
    assert seq % TOKEN_TILE == 0 and seq % ATTN_Q_TILE == 0 and seq % CUMSUM_BLOCK == 0
    assert w_in.shape[-1] == 3 * fox_w + n_heads + 3 * conv_wd + gate_w and n_heads <= LANES

    f0 = 3 * fox_w
    w_in_p = jnp.concatenate(
        [w_in[..., :f0], w_in[..., f0 + n_heads:], w_in[..., f0:f0 + n_heads],
         jnp.zeros((n_layers, d_model, LANES - n_heads), w_in.dtype)], axis=-1).astype(_BF16)
    bf_pad = jnp.pad(b_f, ((0, 0), (0, LANES - n_heads)))
    w_up_a_bf, w_up_b_bf, w_o_bf = (w.astype(_BF16) for w in (w_up_a, w_up_b, w_o))
    w_cq_bf, w_ckv_bf, w_co_bf = (w.astype(_BF16) for w in (w_cq, w_ckv, w_co))
    w_gu_bf, w_down_bf = w_gu.astype(_BF16), w_down.astype(_BF16)
    qg = jnp.tile(q_gain, (1, n_heads)) * (head_dim ** -0.5 * LOG2E)
    kg = jnp.tile(k_gain, (1, n_heads))

    ids = np.arange(fox_w) // head_dim
    gsum = jnp.asarray(ids[:, None] == ids[None, :], _BF16)
    decay_consts = _decay_constants(n_heads, head_dim)

    kmem, vmem = _memkv(mem, norm_mem_kv, w_ckv_bf, ck_gain)

    h = x.reshape(batch * seq, d_model)
    dims = (fox_w, conv_wd, gate_w, head_dim)
    for l in range(n_layers):
        qkv, logf, cc, gates = _inproj(h, l, norm_mix[l][None], w_in_p, bf_pad[l][None],
                                       qg[l][None], kg[l][None], conv_w[l], gsum,
                                       batch=batch, seq=seq, dims=dims)
        eq, ek = _decay(logf, decay_consts, batch=batch, seq=seq)
        a_out = _fox(qkv, eq, ek, batch=batch, seq=seq, n_heads=n_heads, head_dim=head_dim)
        h = _post(a_out, cc, gates, h, l, w_up_a_bf, w_up_b_bf, w_o_bf, norm_mem_q[l][None],
                  w_cq_bf, cq_gain[l][None], kmem, vmem, w_co_bf, batch=batch, seq=seq)
        h = _ffn(h, l, norm_ffn[l][None], w_gu_bf, w_down_bf)
    return h.reshape(batch, seq, d_model)
```

```python
import functools

import jax
import jax.numpy as jnp
import numpy as np
from jax import lax
from jax.experimental import pallas as pl
from jax.experimental.pallas import tpu as pltpu

_BF16 = jnp.bfloat16
_F32 = jnp.float32

EPS = 1e-6
LOG2E = 1.4426950408889634
LANES = 128
SUBLANES = 8
VMEM_LIMIT_BYTES = 56 * 1024 * 1024
MASK_VALUE = -1e30
TOKEN_TILE = 512
ATTN_Q_TILE = 512
ATTN_K_TILE = 512
CUMSUM_BLOCK = 256
FFN_CHUNK = 256


def _dot(a, b):
    return jnp.dot(a, b, preferred_element_type=_F32)


def _dot_nt(a, b):
    return lax.dot_general(a, b, (((1,), (1,)), ((), ())), preferred_element_type=_F32)


def _rms_rows(x, gain):
    ms = jnp.mean(x * x, axis=-1, keepdims=True)
    return x * lax.rsqrt(ms + EPS) * gain


def _split3(x):
    hi = x.astype(_BF16)
    r1 = x - hi.astype(_F32)
    mid = r1.astype(_BF16)
    lo = (r1 - mid.astype(_F32)).astype(_BF16)
    return hi, mid, lo


def _params(*sem):
    return pltpu.CompilerParams(dimension_semantics=sem, vmem_limit_bytes=VMEM_LIMIT_BYTES)


def _resident(block_shape, index_map):
    return pl.BlockSpec(block_shape, index_map, pipeline_mode=pl.Buffered(1))


def _memkv_kernel(mem_ref, g_ref, w_ref, ckg_ref, k_ref, v_ref, *, n_heads, head_dim):
    u = _rms_rows(mem_ref[...], g_ref[...]).astype(_BF16)
    kv = _dot(u, w_ref[...])
    width = n_heads * head_dim
    for e in range(n_heads):
        ke = kv[:, e * head_dim:(e + 1) * head_dim]
        ms = jnp.mean(ke * ke, axis=-1, keepdims=True)
        k_ref[:, e * head_dim:(e + 1) * head_dim] = (
            ke * lax.rsqrt(ms + EPS) * ckg_ref[...]).astype(_BF16)
    v_ref[...] = kv[:, width:].astype(_BF16)


def _memkv(mem, norm_mem_kv, w_ckv_bf, ck_gain):
    n_layers, d_model, two_w = w_ckv_bf.shape
    batch, m_tok, _ = mem.shape
    width = two_w // 2
    head_dim = ck_gain.shape[-1]
    n_heads = width // head_dim
    out = jax.ShapeDtypeStruct((n_layers, batch, m_tok, width), _BF16)
    return pl.pallas_call(
        functools.partial(_memkv_kernel, n_heads=n_heads, head_dim=head_dim),
        grid=(n_layers, batch),
        in_specs=[
            pl.BlockSpec((None, m_tok, d_model), lambda l, b: (b, 0, 0)),
            pl.BlockSpec((None, 1, d_model), lambda l, b: (l, 0, 0)),
            pl.BlockSpec((None, d_model, two_w), lambda l, b: (l, 0, 0)),
            pl.BlockSpec((None, 1, head_dim), lambda l, b: (l, 0, 0)),
        ],
        out_specs=[
            pl.BlockSpec((None, None, m_tok, width), lambda l, b: (l, b, 0, 0)),
            pl.BlockSpec((None, None, m_tok, width), lambda l, b: (l, b, 0, 0)),
        ],
        out_shape=[out, out],
        compiler_params=_params("arbitrary", "arbitrary"),
        name="memkv",
    )(mem, norm_mem_kv[:, None, :], w_ckv_bf, ck_gain[:, None, :])


def _inproj_kernel(h_ref, g_ref, w_ref, bf_ref, qg_ref, kg_ref, cw_ref, gsum_ref,
                   qkv_ref, lf_ref, cc_ref, gate_ref, zc_s,
                   *, tm, fox_w, conv_w, gate_w, head_dim):
    j = pl.program_id(1)
    u = _rms_rows(h_ref[...], g_ref[...]).astype(_BF16)

    for idx, gain_ref in ((0, qg_ref), (1, kg_ref)):
        a = _dot(u, w_ref[:, idx * fox_w:(idx + 1) * fox_w])
        ss = _dot((a * a).astype(_BF16), gsum_ref[...])
        qkv_ref[:, idx * fox_w:(idx + 1) * fox_w] = (
            a * lax.rsqrt(ss * (1.0 / head_dim) + EPS) * gain_ref[...]).astype(_BF16)
    qkv_ref[:, 2 * fox_w:3 * fox_w] = _dot(u, w_ref[:, 2 * fox_w:3 * fox_w]).astype(_BF16)

    o = 3 * fox_w
    z = _dot(u, w_ref[:, o:o + conv_w])
    gate_b = _dot(u, w_ref[:, o + conv_w:o + 2 * conv_w])
    gate_c = _dot(u, w_ref[:, o + 2 * conv_w:o + 3 * conv_w])
    zc = gate_c * z

    @pl.when(j == 0)
    def _():
        zc_s[0:SUBLANES, :] = jnp.zeros((SUBLANES, conv_w), _F32)

    zc_s[SUBLANES:SUBLANES + tm, :] = zc
    y = (cw_ref[2:3, :] * zc
         + cw_ref[1:2, :] * zc_s[SUBLANES - 1:SUBLANES - 1 + tm, :]
         + cw_ref[0:1, :] * zc_s[SUBLANES - 2:SUBLANES - 2 + tm, :])
    cc_ref[...] = (gate_b * y).astype(_BF16)
    zc_s[0:SUBLANES, :] = zc_s[tm:tm + SUBLANES, :]

    o += 3 * conv_w
    step = conv_w
    for i in range(gate_w // step):
        a = _dot(u, w_ref[:, o + i * step:o + (i + 1) * step])
        gate_ref[:, i * step:(i + 1) * step] = jax.nn.sigmoid(a).astype(_BF16)

    o += gate_w
    a = _dot(u, w_ref[:, o:o + LANES]) + bf_ref[...]
    lf_ref[...] = jnp.minimum(a, 0.0) - jnp.log1p(jnp.exp(-jnp.abs(a)))


def _inproj(h, layer, gain, w_in_p, bf_pad, qg, kg, conv_w_l, gsum, *, batch, seq, dims):
    fox_w, conv_wd, gate_w, head_dim = dims
    n_tok, d_model = h.shape
    tm = TOKEN_TILE
    nt = seq // tm
    in_w = w_in_p.shape[-1]
    row = lambda b, j: (b * nt + j, 0)
    const2 = lambda b, j: (0, 0)
    return pl.pallas_call(
        functools.partial(_inproj_kernel, tm=tm, fox_w=fox_w, conv_w=conv_wd, gate_w=gate_w,
                          head_dim=head_dim),
        grid=(batch, nt),
        in_specs=[
            pl.BlockSpec((tm, d_model), row),
            _resident((1, d_model), const2),
            _resident((None, d_model, in_w), lambda b, j: (layer, 0, 0)),
            _resident((1, LANES), const2),
            _resident((1, fox_w), const2),
            _resident((1, fox_w), const2),
            _resident(conv_w_l.shape, const2),
            _resident(gsum.shape, const2),
        ],
        out_specs=[
            pl.BlockSpec((tm, 3 * fox_w), row),
            pl.BlockSpec((tm, LANES), row),
            pl.BlockSpec((tm, conv_wd), row),
            pl.BlockSpec((tm, gate_w), row),
        ],
        out_shape=[
            jax.ShapeDtypeStruct((n_tok, 3 * fox_w), _BF16),
            jax.ShapeDtypeStruct((n_tok, LANES), _F32),
            jax.ShapeDtypeStruct((n_tok, conv_wd), _BF16),
            jax.ShapeDtypeStruct((n_tok, gate_w), _BF16),
        ],
        scratch_shapes=[pltpu.VMEM((tm + SUBLANES, conv_wd), _F32)],
        compiler_params=_params("arbitrary", "arbitrary"),
        name="inproj",
    )(h, gain, w_in_p, bf_pad, qg, kg, conv_w_l, gsum)


def _decay_kernel(lf_ref, tri_ref, sq_ref, sk_ref, constq_ref, constk_ref, eq_ref, ek_ref,
                  *, seq, blk):
    tri = tri_ref[...]
    carry = jnp.zeros((1, LANES), _F32)
    for i in range(seq // blk):
        rows = slice(i * blk, (i + 1) * blk)
        hi, mid, lo = _split3(lf_ref[rows, :])
        c = (_dot(tri, lo) + _dot(tri, mid)) + _dot(tri, hi) + carry
        carry = c[blk - 1:blk, :]
        parts = jnp.concatenate(_split3(c * LOG2E), axis=1)
        eq_ref[rows, :] = (_dot(parts, sq_ref[...]) + constq_ref[...]).astype(_BF16)
        ek_ref[rows, :] = (_dot(parts, sk_ref[...]) + constk_ref[...]).astype(_BF16)


def _decay(logf, consts, *, batch, seq):
    tri, sq = consts[0], consts[1]
    blk = tri.shape[0]
    aug_w = sq.shape[1]
    n_tok = logf.shape[0]
    const = lambda b: (0, 0)
    out = jax.ShapeDtypeStruct((n_tok, aug_w), _BF16)
    return pl.pallas_call(
        functools.partial(_decay_kernel, seq=seq, blk=blk),
        grid=(batch,),
        in_specs=[pl.BlockSpec((seq, LANES), lambda b: (b, 0))]
        + [_resident(c.shape, const) for c in consts],
        out_specs=[pl.BlockSpec((seq, aug_w), lambda b: (b, 0))] * 2,
        out_shape=[out, out],
        compiler_params=_params("arbitrary"),
        name="decay",
    )(logf, *consts)


def _decay_constants(n_heads, head_dim):
    n_parts = 3
    r = np.arange(CUMSUM_BLOCK)
    tri = (r[:, None] >= r[None, :]).astype(np.float32)
    sq = np.zeros((n_parts * LANES, n_heads * LANES), np.float32)
    sk = np.zeros_like(sq)
    constq = np.zeros((1, n_heads * LANES), np.float32)
    constk = np.zeros_like(constq)
    for h in range(n_heads):
        base = h * LANES + (head_dim if h % 2 == 0 else 0)
        for part in range(n_parts):
            sq[part * LANES + h, base + part] = 1.0
            sk[part * LANES + h, base + n_parts + part] = -1.0
            constq[0, base + n_parts + part] = 1.0
            constk[0, base + part] = 1.0
    return (jnp.asarray(tri, _BF16), jnp.asarray(sq, _BF16), jnp.asarray(sk, _BF16),
            jnp.asarray(constq), jnp.asarray(constk))


def _fox_kernel(q_ref, eq0_ref, eq1_ref, k_ref, ek0_ref, ek1_ref, v_ref, o_ref,
                *, seq, tq, tk, head_dim):
    lane = lax.broadcasted_iota(jnp.int32, (1, 2 * head_dim), 1)
    first = lane < head_dim
    rows = lax.broadcasted_iota(jnp.int32, (tq, tk), 0)
    cols = lax.broadcasted_iota(jnp.int32, (tq, tk), 1)
    causal = rows >= cols
    pair_w = 2 * head_dim
    ones_col = jnp.where(lax.broadcasted_iota(jnp.int32, (tk, LANES), 1) == 0, 1.0, 0.0).astype(_BF16)

    for qi in range(seq // tq):
        qs = slice(qi * tq, (qi + 1) * tq)
        q = q_ref[qs, :]
        q_aug = (jnp.where(first, q, eq0_ref[qs, :]), jnp.where(first, eq1_ref[qs, :], q))
        state = [(jnp.full((tq, 1), MASK_VALUE, _F32), jnp.zeros((tq, pair_w + LANES), _F32))
                 for _ in range(2)]
        for j in range(qi + 1):
            ks = slice(j * tk, (j + 1) * tk)
            kc = k_ref[ks, :]
            v_aug = jnp.concatenate([v_ref[ks, :], ones_col], axis=1)
            k_aug = (jnp.where(first, kc, ek0_ref[ks, :]), jnp.where(first, ek1_ref[ks, :], kc))
            for e in range(2):
                m, acc = state[e]
                s = _dot_nt(q_aug[e], k_aug[e])
                if j == qi:
                    s = jnp.where(causal, s, MASK_VALUE)
                m_new = jnp.maximum(m, jnp.max(s, axis=1, keepdims=True))
                alpha = jnp.exp2(m - m_new)
                p = jnp.exp2(s - m_new).astype(_BF16)
                state[e] = (m_new, alpha * acc + _dot(p, v_aug))
        (_, acc0), (_, acc1) = state
        o_ref[qs, :] = jnp.where(first, acc0[:, :pair_w] / acc0[:, pair_w:pair_w + 1],
                                 acc1[:, :pair_w] / acc1[:, pair_w:pair_w + 1]).astype(_BF16)


def _fox(qkv, eq, ek, *, batch, seq, n_heads, head_dim):
    n_tok = qkv.shape[0]
    fox_w = n_heads * head_dim
    n_pairs = n_heads // 2
    pair_w = 2 * head_dim
    tq, tk = ATTN_Q_TILE, ATTN_K_TILE
    assert tq == tk and pair_w == LANES
    return pl.pallas_call(
        functools.partial(_fox_kernel, seq=seq, tq=tq, tk=tk, head_dim=head_dim),
        grid=(batch, n_pairs),
        in_specs=[
            pl.BlockSpec((seq, pair_w), lambda b, p: (b, p)),
            pl.BlockSpec((seq, LANES), lambda b, p: (b, 2 * p)),
            pl.BlockSpec((seq, LANES), lambda b, p: (b, 2 * p + 1)),
            pl.BlockSpec((seq, pair_w), lambda b, p: (b, n_pairs + p)),
            pl.BlockSpec((seq, LANES), lambda b, p: (b, 2 * p)),
            pl.BlockSpec((seq, LANES), lambda b, p: (b, 2 * p + 1)),
            pl.BlockSpec((seq, pair_w), lambda b, p: (b, 2 * n_pairs + p)),
        ],
        out_specs=pl.BlockSpec((seq, pair_w), lambda b, p: (b, p)),
        out_shape=jax.ShapeDtypeStruct((n_tok, fox_w), _BF16),
        compiler_params=_params("arbitrary", "arbitrary"),
        name="fox",
    )(qkv, eq, eq, qkv, ek, ek, qkv)


def _post_kernel(a_ref, cc_ref, gate_ref, h_ref, wua_ref, wub_ref, wo_ref, gq_ref, wcq_ref,
                 cqg_ref, km_ref, vm_ref, wco_ref, o_ref, *, n_heads, head_dim):
    d_model = h_ref.shape[-1]
    up_a = _dot(a_ref[...], wua_ref[...])
    up_b = _dot(cc_ref[...], wub_ref[...])
    merged = (gate_ref[:, :d_model].astype(_F32) * up_a
              + gate_ref[:, d_model:].astype(_F32) * up_b)
    h1 = h_ref[...] + _dot(merged.astype(_BF16), wo_ref[...])

    u = _rms_rows(h1, gq_ref[...]).astype(_BF16)
    qm = _dot(u, wcq_ref[...])
    scale = head_dim ** -0.5
    heads = []
    for e in range(n_heads):
        sl = slice(e * head_dim, (e + 1) * head_dim)
        qe = qm[:, sl]
        ms = jnp.mean(qe * qe, axis=-1, keepdims=True)
        qn = (qe * lax.rsqrt(ms + EPS) * cqg_ref[...]).astype(_BF16)
        s = _dot_nt(qn, km_ref[:, sl]) * scale
        p = jnp.exp(s - jnp.max(s, axis=1, keepdims=True))
        den = jnp.sum(p, axis=1, keepdims=True)
        heads.append((_dot(p.astype(_BF16), vm_ref[:, sl]) / den).astype(_BF16))
    o_ref[...] = h1 + _dot(jnp.concatenate(heads, axis=1), wco_ref[...])


def _post(a_out, cc, gates, h, layer, w_up_a, w_up_b, w_o, gq, w_cq, cq_gain_l, kmem, vmem, w_co,
          *, batch, seq):
    n_tok, d_model = h.shape
    tm = TOKEN_TILE
    nt = seq // tm
    fox_w = a_out.shape[1]
    conv_wd = cc.shape[1]
    m_tok, mem_w = kmem.shape[2], kmem.shape[3]
    head_dim = cq_gain_l.shape[-1]
    row = lambda b, j: (b * nt + j, 0)
    const2 = lambda b, j: (0, 0)
    lay3 = lambda b, j: (layer, 0, 0)
    return pl.pallas_call(
        functools.partial(_post_kernel, n_heads=mem_w // head_dim, head_dim=head_dim),
        grid=(batch, nt),
        in_specs=[
            pl.BlockSpec((tm, fox_w), row),
            pl.BlockSpec((tm, conv_wd), row),
            pl.BlockSpec((tm, 2 * d_model), row),
            pl.BlockSpec((tm, d_model), row),
            _resident((None, fox_w, d_model), lay3),
            _resident((None, conv_wd, d_model), lay3),
            _resident((None, d_model, d_model), lay3),
            _resident((1, d_model), const2),
            _resident((None, d_model, mem_w), lay3),
            _resident((1, head_dim), const2),
            pl.BlockSpec((None, None, m_tok, mem_w), lambda b, j: (layer, b, 0, 0)),
            pl.BlockSpec((None, None, m_tok, mem_w), lambda b, j: (layer, b, 0, 0)),
            _resident((None, mem_w, d_model), lay3),
        ],
        out_specs=pl.BlockSpec((tm, d_model), row),
        out_shape=jax.ShapeDtypeStruct((n_tok, d_model), _F32),
        compiler_params=_params("arbitrary", "arbitrary"),
        name="post",
    )(a_out, cc, gates, h, w_up_a, w_up_b, w_o, gq, w_cq, cq_gain_l, kmem, vmem, w_co)


def _ffn_kernel(h_ref, g_ref, wgu_ref, wd_ref, o_ref, act_s, *, d_ff, chunk):
    h = h_ref[...]
    u = _rms_rows(h, g_ref[...]).astype(_BF16)
    for c in range(d_ff // chunk):
        gate = _dot(u, wgu_ref[:, c * chunk:(c + 1) * chunk])
        up = _dot(u, wgu_ref[:, d_ff + c * chunk:d_ff + (c + 1) * chunk])
        act_s[:, c * chunk:(c + 1) * chunk] = (gate * jax.nn.sigmoid(gate) * up).astype(_BF16)
    o_ref[...] = h + _dot(act_s[...], wd_ref[...])


def _ffn(h, layer, gain, w_gu, w_down):
    n_tok, d_model = h.shape
    d_ff = w_down.shape[1]
    tm = TOKEN_TILE
    assert d_ff % FFN_CHUNK == 0
    return pl.pallas_call(
        functools.partial(_ffn_kernel, d_ff=d_ff, chunk=FFN_CHUNK),
        grid=(n_tok // tm,),
        in_specs=[
            pl.BlockSpec((tm, d_model), lambda i: (i, 0)),
            _resident((1, d_model), lambda i: (0, 0)),
            _resident((None, d_model, 2 * d_ff), lambda i: (layer, 0, 0)),
            _resident((None, d_ff, d_model), lambda i: (layer, 0, 0)),
        ],
        out_specs=pl.BlockSpec((tm, d_model), lambda i: (i, 0)),
        out_shape=jax.ShapeDtypeStruct((n_tok, d_model), _F32),
        scratch_shapes=[pltpu.VMEM((tm, d_ff), _BF16)],
        compiler_params=_params("arbitrary"),
        name="ffn",
    )(h, gain, w_gu, w_down)


def kernel(x, mem, norm_mix, w_in, b_f, q_gain, k_gain, conv_w, w_up_a, w_up_b, w_o,
           norm_mem_q, norm_mem_kv, w_cq, w_ckv, cq_gain, ck_gain, w_co, norm_ffn, w_gu, w_down):
    batch, seq, d_model = x.shape
    n_layers = w_in.shape[0]
    head_dim = q_gain.shape[-1]
    n_heads = b_f.shape[-1]
    fox_w = n_heads * head_dim
    conv_wd = conv_w.shape[-1]
    gate_w = 2 * d_model
    n_pairs = n_heads // 2
    assert seq % TOKEN_TILE == 0 and seq % ATTN_Q_TILE == 0 and seq % CUMSUM_BLOCK == 0
    assert w_in.shape[-1] == 3 * fox_w + n_heads + 3 * conv_wd + gate_w and n_heads <= LANES

    f0 = 3 * fox_w
    w_in_p = jnp.concatenate(
        [w_in[..., :f0], w_in[..., f0 + n_heads:], w_in[..., f0:f0 + n_heads],
         jnp.zeros((n_layers, d_model, LANES - n_heads), w_in.dtype)], axis=-1).astype(_BF16)
    bf_pad = jnp.pad(b_f, ((0, 0), (0, LANES - n_heads)))
    w_up_a_bf, w_up_b_bf, w_o_bf = (w.astype(_BF16) for w in (w_up_a, w_up_b, w_o))
    w_cq_bf, w_ckv_bf, w_co_bf = (w.astype(_BF16) for w in (w_cq, w_ckv, w_co))
    w_gu_bf, w_down_bf = w_gu.astype(_BF16), w_down.astype(_BF16)
    qg = jnp.tile(q_gain, (1, n_heads)) * (head_dim ** -0.5 * LOG2E)
    kg = jnp.tile(k_gain, (1, n_heads))

    ids = np.arange(fox_w) // head_dim
    gsum = jnp.asarray(ids[:, None] == ids[None, :], _BF16)
    decay_consts = _decay_constants(n_heads, head_dim)

    kmem, vmem = _memkv(mem, norm_mem_kv, w_ckv_bf, ck_gain)

    h = x.reshape(batch * seq, d_model)
    dims = (fox_w, conv_wd, gate_w, head_dim)
    for l in range(n_layers):
        qkv, logf, cc, gates = _inproj(h, l, norm_mix[l][None], w_in_p, bf_pad[l][None],
                                       qg[l][None], kg[l][None], conv_w[l], gsum,
                                       batch=batch, seq=seq, dims=dims)
        eq, ek = _decay(logf, decay_consts, batch=batch, seq=seq)
        a_out = _fox(qkv, eq, ek, batch=batch, seq=seq, n_heads=n_heads, head_dim=head_dim)
        h = _post(a_out, cc, gates, h, l, w_up_a_bf, w_up_b_bf, w_o_bf, norm_mem_q[l][None],
                  w_cq_bf, cq_gain[l][None], kmem, vmem, w_co_bf, batch=batch, seq=seq)
        h = _ffn(h, l, norm_ffn[l][None], w_gu_bf, w_down_bf)
    return h.reshape(batch, seq, d_model)
```

```python
import functools

import jax
import jax.numpy as jnp
import numpy as np
from jax import lax
from jax.experimental import pallas as pl
from jax.experimental.pallas import tpu as pltpu

_BF16 = jnp.bfloat16
_F32 = jnp.float32

EPS = 1e-6
LOG2E = 1.4426950408889634
LANES = 128
SUBLANES = 8
VMEM_LIMIT_BYTES = 56 * 1024 * 1024
MASK_VALUE = -1e30
TOKEN_TILE = 512
ATTN_Q_TILE = 512
ATTN_K_TILE = 512
CUMSUM_BLOCK = 256
DECAY_PARTS = 3
FFN_CHUNK = 256


def _dot(a, b):
    return jnp.dot(a, b, preferred_element_type=_F32)


def _dot_nt(a, b):
    return lax.dot_general(a, b, (((1,), (1,)), ((), ())), preferred_element_type=_F32)


def _rms_rows(x, gain):
    ms = jnp.mean(x * x, axis=-1, keepdims=True)
    return x * lax.rsqrt(ms + EPS) * gain


def _split3(x):
    hi = x.astype(_BF16)
    r1 = x - hi.astype(_F32)
    mid = r1.astype(_BF16)
    lo = (r1 - mid.astype(_F32)).astype(_BF16)
    return hi, mid, lo


def _params(*sem):
    return pltpu.CompilerParams(dimension_semantics=sem, vmem_limit_bytes=VMEM_LIMIT_BYTES)


def _resident(block_shape, index_map):
    return pl.BlockSpec(block_shape, index_map, pipeline_mode=pl.Buffered(1))


def _memkv_kernel(mem_ref, g_ref, w_ref, ckg_ref, k_ref, v_ref, *, n_heads, head_dim):
    u = _rms_rows(mem_ref[...], g_ref[...]).astype(_BF16)
    kv = _dot(u, w_ref[...])
    width = n_heads * head_dim
    for e in range(n_heads):
        ke = kv[:, e * head_dim:(e + 1) * head_dim]
        ms = jnp.mean(ke * ke, axis=-1, keepdims=True)
        k_ref[:, e * head_dim:(e + 1) * head_dim] = (
            ke * lax.rsqrt(ms + EPS) * ckg_ref[...]).astype(_BF16)
    v_ref[...] = kv[:, width:].astype(_BF16)


def _memkv(mem, norm_mem_kv, w_ckv_bf, ck_gain):
    n_layers, d_model, two_w = w_ckv_bf.shape
    batch, m_tok, _ = mem.shape
    width = two_w // 2
    head_dim = ck_gain.shape[-1]
    n_heads = width // head_dim
    out = jax.ShapeDtypeStruct((n_layers, batch, m_tok, width), _BF16)
    return pl.pallas_call(
        functools.partial(_memkv_kernel, n_heads=n_heads, head_dim=head_dim),
        grid=(n_layers, batch),
        in_specs=[
            pl.BlockSpec((None, m_tok, d_model), lambda l, b: (b, 0, 0)),
            pl.BlockSpec((None, 1, d_model), lambda l, b: (l, 0, 0)),
            pl.BlockSpec((None, d_model, two_w), lambda l, b: (l, 0, 0)),
            pl.BlockSpec((None, 1, head_dim), lambda l, b: (l, 0, 0)),
        ],
        out_specs=[
            pl.BlockSpec((None, None, m_tok, width), lambda l, b: (l, b, 0, 0)),
            pl.BlockSpec((None, None, m_tok, width), lambda l, b: (l, b, 0, 0)),
        ],
        out_shape=[out, out],
        compiler_params=_params("arbitrary", "arbitrary"),
        name="memkv",
    )(mem, norm_mem_kv[:, None, :], w_ckv_bf, ck_gain[:, None, :])


def _inproj_kernel(h_ref, g_ref, wqkv_ref, wrest_ref, wf_ref, bf_ref, qg_ref, kg_ref, cw_ref,
                   gsum_ref, qkv_ref, lf_ref, cc_ref, gate_ref, zc_s,
                   *, tm, fox_w, conv_w, gate_w, head_dim):
    j = pl.program_id(1)
    u = _rms_rows(h_ref[...], g_ref[...]).astype(_BF16)

    for idx, gain_ref in ((0, qg_ref), (1, kg_ref)):
        a = _dot(u, wqkv_ref[:, idx * fox_w:(idx + 1) * fox_w])
        ss = _dot((a * a).astype(_BF16), gsum_ref[...])
        qkv_ref[:, idx * fox_w:(idx + 1) * fox_w] = (
            a * lax.rsqrt(ss * (1.0 / head_dim) + EPS) * gain_ref[...]).astype(_BF16)
    qkv_ref[:, 2 * fox_w:3 * fox_w] = _dot(u, wqkv_ref[:, 2 * fox_w:3 * fox_w]).astype(_BF16)

    z = _dot(u, wrest_ref[:, 0:conv_w])
    gate_b = _dot(u, wrest_ref[:, conv_w:2 * conv_w])
    gate_c = _dot(u, wrest_ref[:, 2 * conv_w:3 * conv_w])
    zc = gate_c * z

    @pl.when(j == 0)
    def _():
        zc_s[0:SUBLANES, :] = jnp.zeros((SUBLANES, conv_w), _F32)

    zc_s[SUBLANES:SUBLANES + tm, :] = zc
    y = (cw_ref[2:3, :] * zc
         + cw_ref[1:2, :] * zc_s[SUBLANES - 1:SUBLANES - 1 + tm, :]
         + cw_ref[0:1, :] * zc_s[SUBLANES - 2:SUBLANES - 2 + tm, :])
    cc_ref[...] = (gate_b * y).astype(_BF16)
    zc_s[0:SUBLANES, :] = zc_s[tm:tm + SUBLANES, :]

    o = 3 * conv_w
    step = conv_w
    for i in range(gate_w // step):
        a = _dot(u, wrest_ref[:, o + i * step:o + (i + 1) * step])
        gate_ref[:, i * step:(i + 1) * step] = jax.nn.sigmoid(a).astype(_BF16)

    a = _dot(u, wf_ref[...]) + bf_ref[...]
    lf_ref[...] = jnp.minimum(a, 0.0) - jnp.log1p(jnp.exp(-jnp.abs(a)))


def _inproj(h, layer, gain, w_qkv, w_rest, w_f, bf_pad, qg, kg, conv_w_l, gsum, *, batch, seq, dims):
    fox_w, conv_wd, gate_w, head_dim = dims
    n_tok, d_model = h.shape
    tm = TOKEN_TILE
    nt = seq // tm
    row = lambda b, j: (b * nt + j, 0)
    const2 = lambda b, j: (0, 0)
    lay3 = lambda b, j: (layer, 0, 0)
    return pl.pallas_call(
        functools.partial(_inproj_kernel, tm=tm, fox_w=fox_w, conv_w=conv_wd, gate_w=gate_w,
                          head_dim=head_dim),
        grid=(batch, nt),
        in_specs=[
            pl.BlockSpec((tm, d_model), row),
            _resident((1, d_model), const2),
            _resident((None, d_model, w_qkv.shape[-1]), lay3),
            _resident((None, d_model, w_rest.shape[-1]), lay3),
            _resident((None, d_model, LANES), lay3),
            _resident((1, LANES), const2),
            _resident((1, fox_w), const2),
            _resident((1, fox_w), const2),
            _resident(conv_w_l.shape, const2),
            _resident(gsum.shape, const2),
        ],
        out_specs=[
            pl.BlockSpec((tm, 3 * fox_w), row),
            pl.BlockSpec((tm, LANES), row),
            pl.BlockSpec((tm, conv_wd), row),
            pl.BlockSpec((tm, gate_w), row),
        ],
        out_shape=[
            jax.ShapeDtypeStruct((n_tok, 3 * fox_w), _BF16),
            jax.ShapeDtypeStruct((n_tok, LANES), _F32),
            jax.ShapeDtypeStruct((n_tok, conv_wd), _BF16),
            jax.ShapeDtypeStruct((n_tok, gate_w), _BF16),
        ],
        scratch_shapes=[pltpu.VMEM((tm + SUBLANES, conv_wd), _F32)],
        compiler_params=_params("arbitrary", "arbitrary"),
        name="inproj",
    )(h, gain, w_qkv, w_rest, w_f, bf_pad, qg, kg, conv_w_l, gsum)


def _decay_kernel(lf_ref, tri_ref, sq_ref, sk_ref, constq_ref, constk_ref, eq_ref, ek_ref,
                  *, seq, blk, n_heads):
    tri = tri_ref[...]
    lane = lax.broadcasted_iota(jnp.int32, (1, LANES), 1)
    carry = jnp.zeros((1, LANES), _F32)
    for i in range(seq // blk):
        rows = slice(i * blk, (i + 1) * blk)
        hi, mid, lo = _split3(lf_ref[rows, :])
        c = (_dot(tri, lo) + _dot(tri, mid)) + _dot(tri, hi) + carry
        carry = c[blk - 1:blk, :]
        hi, mid, lo = _split3(c * LOG2E)
        parts = jnp.where(lane < n_heads, hi, jnp.where(lane < 2 * n_heads, mid, lo))
        eq_ref[rows, :] = (_dot(parts, sq_ref[...]) + constq_ref[...]).astype(_BF16)
        ek_ref[rows, :] = (_dot(parts, sk_ref[...]) + constk_ref[...]).astype(_BF16)


def _decay(logf, consts, *, batch, seq, n_heads):
    tri, sq = consts[0], consts[1]
    blk = tri.shape[0]
    aug_w = sq.shape[1]
    n_tok = logf.shape[0]
    const = lambda b: (0, 0)
    out = jax.ShapeDtypeStruct((n_tok, aug_w), _BF16)
    return pl.pallas_call(
        functools.partial(_decay_kernel, seq=seq, blk=blk, n_heads=n_heads),
        grid=(batch,),
        in_specs=[pl.BlockSpec((seq, LANES), lambda b: (b, 0))]
        + [_resident(c.shape, const) for c in consts],
        out_specs=[pl.BlockSpec((seq, aug_w), lambda b: (b, 0))] * 2,
        out_shape=[out, out],
        compiler_params=_params("arbitrary"),
        name="decay",
    )(logf, *consts)


def _decay_constants(n_heads, head_dim):
    assert DECAY_PARTS * n_heads <= LANES and 2 * DECAY_PARTS <= head_dim
    r = np.arange(CUMSUM_BLOCK)
    tri = (r[:, None] >= r[None, :]).astype(np.float32)
    width = (n_heads // 2) * LANES
    sq = np.zeros((LANES, width), np.float32)
    sk = np.zeros_like(sq)
    constq = np.zeros((1, width), np.float32)
    constk = np.zeros_like(constq)
    for h in range(n_heads):
        base = (h // 2) * LANES + (head_dim if h % 2 == 0 else 0)
        for part in range(DECAY_PARTS):
            sq[part * n_heads + h, base + part] = 1.0
            sk[part * n_heads + h, base + DECAY_PARTS + part] = -1.0
            constq[0, base + DECAY_PARTS + part] = 1.0
            constk[0, base + part] = 1.0
    return (jnp.asarray(tri, _BF16), jnp.asarray(sq, _BF16), jnp.asarray(sk, _BF16),
            jnp.asarray(constq), jnp.asarray(constk))


def _fox_kernel(q_ref, eq_ref, k_ref, ek_ref, v_ref, o_ref, *, seq, tq, tk, head_dim):
    lane = lax.broadcasted_iota(jnp.int32, (1, 2 * head_dim), 1)
    first = lane < head_dim
    rows = lax.broadcasted_iota(jnp.int32, (tq, tk), 0)
    cols = lax.broadcasted_iota(jnp.int32, (tq, tk), 1)
    causal = rows >= cols
    pair_w = 2 * head_dim
    ones_col = jnp.where(lax.broadcasted_iota(jnp.int32, (tk, LANES), 1) == 0, 1.0, 0.0).astype(_BF16)

    for qi in range(seq // tq):
        qs = slice(qi * tq, (qi + 1) * tq)
        q = q_ref[qs, :]
        eq = eq_ref[qs, :]
        q_aug = (jnp.where(first, q, eq), jnp.where(first, eq, q))
        state = [(jnp.full((tq, 1), MASK_VALUE, _F32), jnp.zeros((tq, pair_w + LANES), _F32))
                 for _ in range(2)]
        for j in range(qi + 1):
            ks = slice(j * tk, (j + 1) * tk)
            kc = k_ref[ks, :]
            v_aug = jnp.concatenate([v_ref[ks, :], ones_col], axis=1)
            ekc = ek_ref[ks, :]
            k_aug = (jnp.where(first, kc, ekc), jnp.where(first, ekc, kc))
            for e in range(2):
                m, acc = state[e]
                s = _dot_nt(q_aug[e], k_aug[e])
                if j == qi:
                    s = jnp.where(causal, s, MASK_VALUE)
                m_new = jnp.maximum(m, jnp.max(s, axis=1, keepdims=True))
                alpha = jnp.exp2(m - m_new)
                p = jnp.exp2(s - m_new).astype(_BF16)
                state[e] = (m_new, alpha * acc + _dot(p, v_aug))
        (_, acc0), (_, acc1) = state
        o_ref[qs, :] = jnp.where(first, acc0[:, :pair_w] / acc0[:, pair_w:pair_w + 1],
                                 acc1[:, :pair_w] / acc1[:, pair_w:pair_w + 1]).astype(_BF16)


def _fox(qkv, eq, ek, *, batch, seq, n_heads, head_dim):
    n_tok = qkv.shape[0]
    fox_w = n_heads * head_dim
    n_pairs = n_heads // 2
    pair_w = 2 * head_dim
    tq, tk = ATTN_Q_TILE, ATTN_K_TILE
    assert tq == tk and pair_w == LANES
    return pl.pallas_call(
        functools.partial(_fox_kernel, seq=seq, tq=tq, tk=tk, head_dim=head_dim),
        grid=(batch, n_pairs),
        in_specs=[
            pl.BlockSpec((seq, pair_w), lambda b, p: (b, p)),
            pl.BlockSpec((seq, LANES), lambda b, p: (b, p)),
            pl.BlockSpec((seq, pair_w), lambda b, p: (b, n_pairs + p)),
            pl.BlockSpec((seq, LANES), lambda b, p: (b, p)),
            pl.BlockSpec((seq, pair_w), lambda b, p: (b, 2 * n_pairs + p)),
        ],
        out_specs=pl.BlockSpec((seq, pair_w), lambda b, p: (b, p)),
        out_shape=jax.ShapeDtypeStruct((n_tok, fox_w), _BF16),
        compiler_params=_params("arbitrary", "arbitrary"),
        name="fox",
    )(qkv, eq, qkv, ek, qkv)


def _post_kernel(a_ref, cc_ref, gate_ref, h_ref, wua_ref, wub_ref, wo_ref, gq_ref, wcq_ref,
                 cqg_ref, km_ref, vm_ref, wco_ref, o_ref, *, n_heads, head_dim):
    d_model = h_ref.shape[-1]
    up_a = _dot(a_ref[...], wua_ref[...])
    up_b = _dot(cc_ref[...], wub_ref[...])
    merged = (gate_ref[:, :d_model].astype(_F32) * up_a
              + gate_ref[:, d_model:].astype(_F32) * up_b)
    h1 = h_ref[...] + _dot(merged.astype(_BF16), wo_ref[...])

    u = _rms_rows(h1, gq_ref[...]).astype(_BF16)
    qm = _dot(u, wcq_ref[...])
    scale = head_dim ** -0.5
    heads = []
    for e in range(n_heads):
        sl = slice(e * head_dim, (e + 1) * head_dim)
        qe = qm[:, sl]
        ms = jnp.mean(qe * qe, axis=-1, keepdims=True)
        qn = (qe * lax.rsqrt(ms + EPS) * cqg_ref[...]).astype(_BF16)
        s = _dot_nt(qn, km_ref[:, sl]) * scale
        p = jnp.exp(s - jnp.max(s, axis=1, keepdims=True))
        den = jnp.sum(p, axis=1, keepdims=True)
        heads.append((_dot(p.astype(_BF16), vm_ref[:, sl]) / den).astype(_BF16))
    o_ref[...] = h1 + _dot(jnp.concatenate(heads, axis=1), wco_ref[...])


def _post(a_out, cc, gates, h, layer, w_up_a, w_up_b, w_o, gq, w_cq, cq_gain_l, kmem, vmem, w_co,
          *, batch, seq):
    n_tok, d_model = h.shape
    tm = TOKEN_TILE
    nt = seq // tm
    fox_w = a_out.shape[1]
    conv_wd = cc.shape[1]
    m_tok, mem_w = kmem.shape[2], kmem.shape[3]
    head_dim = cq_gain_l.shape[-1]
    row = lambda b, j: (b * nt + j, 0)
    const2 = lambda b, j: (0, 0)
    lay3 = lambda b, j: (layer, 0, 0)
    return pl.pallas_call(
        functools.partial(_post_kernel, n_heads=mem_w // head_dim, head_dim=head_dim),
        grid=(batch, nt),
        in_specs=[
            pl.BlockSpec((tm, fox_w), row),
            pl.BlockSpec((tm, conv_wd), row),
            pl.BlockSpec((tm, 2 * d_model), row),
            pl.BlockSpec((tm, d_model), row),
            _resident((None, fox_w, d_model), lay3),
            _resident((None, conv_wd, d_model), lay3),
            _resident((None, d_model, d_model), lay3),
            _resident((1, d_model), const2),
            _resident((None, d_model, mem_w), lay3),
            _resident((1, head_dim), const2),
            pl.BlockSpec((None, None, m_tok, mem_w), lambda b, j: (layer, b, 0, 0)),
            pl.BlockSpec((None, None, m_tok, mem_w), lambda b, j: (layer, b, 0, 0)),
            _resident((None, mem_w, d_model), lay3),
        ],
        out_specs=pl.BlockSpec((tm, d_model), row),
        out_shape=jax.ShapeDtypeStruct((n_tok, d_model), _F32),
        compiler_params=_params("arbitrary", "arbitrary"),
        name="post",
    )(a_out, cc, gates, h, w_up_a, w_up_b, w_o, gq, w_cq, cq_gain_l, kmem, vmem, w_co)


def _ffn_kernel(h_ref, g_ref, wgu_ref, wd_ref, o_ref, act_s, *, d_ff, chunk):
    h = h_ref[...]
    u = _rms_rows(h, g_ref[...]).astype(_BF16)
    for c in range(d_ff // chunk):
        gate = _dot(u, wgu_ref[:, c * chunk:(c + 1) * chunk])
        up = _dot(u, wgu_ref[:, d_ff + c * chunk:d_ff + (c + 1) * chunk])
        act_s[:, c * chunk:(c + 1) * chunk] = (gate * jax.nn.sigmoid(gate) * up).astype(_BF16)
    o_ref[...] = h + _dot(act_s[...], wd_ref[...])


def _ffn(h, layer, gain, w_gu, w_down):
    n_tok, d_model = h.shape
    d_ff = w_down.shape[1]
    tm = TOKEN_TILE
    assert d_ff % FFN_CHUNK == 0
    return pl.pallas_call(
        functools.partial(_ffn_kernel, d_ff=d_ff, chunk=FFN_CHUNK),
        grid=(n_tok // tm,),
        in_specs=[
            pl.BlockSpec((tm, d_model), lambda i: (i, 0)),
            _resident((1, d_model), lambda i: (0, 0)),
            _resident((None, d_model, 2 * d_ff), lambda i: (layer, 0, 0)),
            _resident((None, d_ff, d_model), lambda i: (layer, 0, 0)),
        ],
        out_specs=pl.BlockSpec((tm, d_model), lambda i: (i, 0)),
        out_shape=jax.ShapeDtypeStruct((n_tok, d_model), _F32),
        scratch_shapes=[pltpu.VMEM((tm, d_ff), _BF16)],
        compiler_params=_params("arbitrary"),
        name="ffn",
    )(h, gain, w_gu, w_down)


def kernel(x, mem, norm_mix, w_in, b_f, q_gain, k_gain, conv_w, w_up_a, w_up_b, w_o,
           norm_mem_q, norm_mem_kv, w_cq, w_ckv, cq_gain, ck_gain, w_co, norm_ffn, w_gu, w_down):
    batch, seq, d_model = x.shape
    n_layers = w_in.shape[0]
    head_dim = q_gain.shape[-1]
    n_heads = b_f.shape[-1]
    fox_w = n_heads * head_dim
    conv_wd = conv_w.shape[-1]
    gate_w = 2 * d_model
    assert seq % TOKEN_TILE == 0 and seq % ATTN_Q_TILE == 0 and seq % CUMSUM_BLOCK == 0
    assert w_in.shape[-1] == 3 * fox_w + n_heads + 3 * conv_wd + gate_w

    f0 = 3 * fox_w
    w_qkv = w_in[..., :f0].astype(_BF16)
    w_rest = w_in[..., f0 + n_heads:].astype(_BF16)
    f_pad = LANES - DECAY_PARTS * n_heads
    w_f = jnp.pad(jnp.tile(w_in[..., f0:f0 + n_heads], (1, 1, DECAY_PARTS)),
                  ((0, 0), (0, 0), (0, f_pad))).astype(_BF16)
    bf_pad = jnp.pad(jnp.tile(b_f, (1, DECAY_PARTS)), ((0, 0), (0, f_pad)))
    w_up_a_bf, w_up_b_bf, w_o_bf = (w.astype(_BF16) for w in (w_up_a, w_up_b, w_o))
    w_cq_bf, w_ckv_bf, w_co_bf = (w.astype(_BF16) for w in (w_cq, w_ckv, w_co))
    w_gu_bf, w_down_bf = w_gu.astype(_BF16), w_down.astype(_BF16)
    qg = jnp.tile(q_gain, (1, n_heads)) * (head_dim ** -0.5 * LOG2E)
    kg = jnp.tile(k_gain, (1, n_heads))

    ids = np.arange(fox_w) // head_dim
    gsum = jnp.asarray(ids[:, None] == ids[None, :], _BF16)
    decay_consts = _decay_constants(n_heads, head_dim)

    kmem, vmem = _memkv(mem, norm_mem_kv, w_ckv_bf, ck_gain)

    h = x.reshape(batch * seq, d_model)
    dims = (fox_w, conv_wd, gate_w, head_dim)
    for l in range(n_layers):
        qkv, logf, cc, gates = _inproj(h, l, norm_mix[l][None], w_qkv, w_rest, w_f, bf_pad[l][None],
                                       qg[l][None], kg[l][None], conv_w[l], gsum,
                                       batch=batch, seq=seq, dims=dims)
        eq, ek = _decay(logf, decay_consts, batch=batch, seq=seq, n_heads=n_heads)
        a_out = _fox(qkv, eq, ek, batch=batch, seq=seq, n_heads=n_heads, head_dim=head_dim)
        h = _post(a_out, cc, gates, h, l, w_up_a_bf, w_up_b_bf, w_o_bf, norm_mem_q[l][None],
                  w_cq_bf, cq_gain[l][None], kmem, vmem, w_co_bf, batch=batch, seq=seq)
        h = _ffn(h, l, norm_ffn[l][None], w_gu_bf, w_down_bf)
    return h.reshape(batch, seq, d_model)
```

```python
import functools

import jax
import jax.numpy as jnp
import numpy as np
from jax import lax
from jax.experimental import pallas as pl
from jax.experimental.pallas import tpu as pltpu

_BF16 = jnp.bfloat16
_F32 = jnp.float32

EPS = 1e-6
LOG2E = 1.4426950408889634
LANES = 128
SUBLANES = 8
VMEM_LIMIT_BYTES = 56 * 1024 * 1024
MASK_VALUE = -1e30
TOKEN_TILE = 512
ATTN_Q_TILE = 512
ATTN_K_TILE = 512
CUMSUM_BLOCK = 256
DECAY_PARTS = 3
FFN_CHUNK = 256


def _dot(a, b):
    return jnp.dot(a, b, preferred_element_type=_F32)


def _dot_nt(a, b):
    return lax.dot_general(a, b, (((1,), (1,)), ((), ())), preferred_element_type=_F32)


def _rms_rows(x, gain):
    ms = jnp.mean(x * x, axis=-1, keepdims=True)
    return x * lax.rsqrt(ms + EPS) * gain


def _split3(x):
    hi = x.astype(_BF16)
    r1 = x - hi.astype(_F32)
    mid = r1.astype(_BF16)
    lo = (r1 - mid.astype(_F32)).astype(_BF16)
    return hi, mid, lo


def _params(*sem):
    return pltpu.CompilerParams(dimension_semantics=sem, vmem_limit_bytes=VMEM_LIMIT_BYTES)


def _resident(block_shape, index_map):
    return pl.BlockSpec(block_shape, index_map, pipeline_mode=pl.Buffered(1))


def _memkv_kernel(mem_ref, g_ref, w_ref, ckg_ref, k_ref, v_ref, *, n_heads, head_dim):
    u = _rms_rows(mem_ref[...], g_ref[...]).astype(_BF16)
    kv = _dot(u, w_ref[...])
    width = n_heads * head_dim
    for e in range(n_heads):
        ke = kv[:, e * head_dim:(e + 1) * head_dim]
        ms = jnp.mean(ke * ke, axis=-1, keepdims=True)
        k_ref[:, e * head_dim:(e + 1) * head_dim] = (
            ke * lax.rsqrt(ms + EPS) * ckg_ref[...]).astype(_BF16)
    v_ref[...] = kv[:, width:].astype(_BF16)


def _memkv(mem, norm_mem_kv, w_ckv_bf, ck_gain):
    n_layers, d_model, two_w = w_ckv_bf.shape
    batch, m_tok, _ = mem.shape
    width = two_w // 2
    head_dim = ck_gain.shape[-1]
    n_heads = width // head_dim
    out = jax.ShapeDtypeStruct((n_layers, batch, m_tok, width), _BF16)
    return pl.pallas_call(
        functools.partial(_memkv_kernel, n_heads=n_heads, head_dim=head_dim),
        grid=(n_layers, batch),
        in_specs=[
            pl.BlockSpec((None, m_tok, d_model), lambda l, b: (b, 0, 0)),
            pl.BlockSpec((None, 1, d_model), lambda l, b: (l, 0, 0)),
            pl.BlockSpec((None, d_model, two_w), lambda l, b: (l, 0, 0)),
            pl.BlockSpec((None, 1, head_dim), lambda l, b: (l, 0, 0)),
        ],
        out_specs=[
            pl.BlockSpec((None, None, m_tok, width), lambda l, b: (l, b, 0, 0)),
            pl.BlockSpec((None, None, m_tok, width), lambda l, b: (l, b, 0, 0)),
        ],
        out_shape=[out, out],
        compiler_params=_params("arbitrary", "arbitrary"),
        name="memkv",
    )(mem, norm_mem_kv[:, None, :], w_ckv_bf, ck_gain[:, None, :])


def _inproj_kernel(h_ref, g_ref, wqkv_ref, wrest_ref, wf_ref, bf_ref, qg_ref, kg_ref, cw_ref,
                   qkv_ref, lf_ref, cc_ref, gate_ref, tail_s,
                   *, tm, fox_w, conv_w, gate_w, head_dim):
    j = pl.program_id(1)
    u = _rms_rows(h_ref[...], g_ref[...]).astype(_BF16)

    assert 2 * head_dim == LANES
    first = lax.broadcasted_iota(jnp.int32, (1, LANES), 1) < head_dim
    for idx, gain_ref in ((0, qg_ref), (1, kg_ref)):
        a = _dot(u, wqkv_ref[:, idx * fox_w:(idx + 1) * fox_w])
        for c in range(fox_w // LANES):
            cs = slice(c * LANES, (c + 1) * LANES)
            ac = a[:, cs]
            sq = ac * ac
            lo = jnp.sum(jnp.where(first, sq, 0.0), axis=1, keepdims=True)
            hi = jnp.sum(jnp.where(first, 0.0, sq), axis=1, keepdims=True)
            ss = jnp.where(first, lo, hi)
            qkv_ref[:, idx * fox_w + cs.start:idx * fox_w + cs.stop] = (
                ac * lax.rsqrt(ss * (1.0 / head_dim) + EPS) * gain_ref[:, cs]).astype(_BF16)
    qkv_ref[:, 2 * fox_w:3 * fox_w] = _dot(u, wqkv_ref[:, 2 * fox_w:3 * fox_w]).astype(_BF16)

    @pl.when(j == 0)
    def _():
        tail_s[...] = jnp.zeros(tail_s.shape, _F32)

    n_taps = cw_ref.shape[0]
    hr = tail_s.shape[0]
    assert n_taps - 1 <= hr and conv_w % (2 * LANES) == 0
    sub = lax.broadcasted_iota(jnp.int32, (hr, LANES), 0)
    for cp in range(conv_w // (2 * LANES)):
        gate_b = _dot(u, wrest_ref[:, conv_w + cp * 2 * LANES:conv_w + (cp + 1) * 2 * LANES])
        for half in range(2):
            cs = slice((2 * cp + half) * LANES, (2 * cp + half + 1) * LANES)
            w_pair = jnp.concatenate(
                [wrest_ref[:, cs], wrest_ref[:, 2 * conv_w + cs.start:2 * conv_w + cs.stop]], axis=1)
            r = _dot(u, w_pair)
            zc = r[:, :LANES] * r[:, LANES:]
            y = cw_ref[n_taps - 1:n_taps, cs] * zc
            head = zc[0:hr, :]
            y_head = cw_ref[n_taps - 1:n_taps, cs] * head
            prev = tail_s[:, cs]
            for d in range(1, n_taps):
                w_d = cw_ref[n_taps - 1 - d:n_taps - d, cs]
                y = y + w_d * pltpu.roll(zc, d, axis=0)
                y_head = y_head + w_d * jnp.where(sub < d, pltpu.roll(prev, d, axis=0),
                                                  pltpu.roll(head, d, axis=0))
            gb = gate_b[:, half * LANES:(half + 1) * LANES]
            cc_ref[0:hr, cs] = (gb[0:hr, :] * y_head).astype(_BF16)
            cc_ref[hr:tm, cs] = (gb[hr:tm, :] * y[hr:tm, :]).astype(_BF16)
            tail_s[:, cs] = zc[tm - hr:tm, :]

    o = 3 * conv_w
    step = conv_w
    for i in range(gate_w // step):
        a = _dot(u, wrest_ref[:, o + i * step:o + (i + 1) * step])
        gate_ref[:, i * step:(i + 1) * step] = jax.nn.sigmoid(a).astype(_BF16)

    a = _dot(u, wf_ref[...]) + bf_ref[...]
    lf_ref[...] = jnp.minimum(a, 0.0) - jnp.log1p(jnp.exp(-jnp.abs(a)))


def _inproj(h, layer, gain, w_qkv, w_rest, w_f, bf_pad, qg, kg, conv_w_l, *, batch, seq, dims):
    fox_w, conv_wd, gate_w, head_dim = dims
    n_tok, d_model = h.shape
    tm = TOKEN_TILE
    nt = seq // tm
    row = lambda b, j: (b * nt + j, 0)
    const2 = lambda b, j: (0, 0)
    lay3 = lambda b, j: (layer, 0, 0)
    return pl.pallas_call(
        functools.partial(_inproj_kernel, tm=tm, fox_w=fox_w, conv_w=conv_wd, gate_w=gate_w,
                          head_dim=head_dim),
        grid=(batch, nt),
        in_specs=[
            pl.BlockSpec((tm, d_model), row),
            _resident((1, d_model), const2),
            _resident((None, d_model, w_qkv.shape[-1]), lay3),
            _resident((None, d_model, w_rest.shape[-1]), lay3),
            _resident((None, d_model, LANES), lay3),
            _resident((1, LANES), const2),
            _resident((1, fox_w), const2),
            _resident((1, fox_w), const2),
            _resident(conv_w_l.shape, const2),
        ],
        out_specs=[
            pl.BlockSpec((tm, 3 * fox_w), row),
            pl.BlockSpec((tm, LANES), row),
            pl.BlockSpec((tm, conv_wd), row),
            pl.BlockSpec((tm, gate_w), row),
        ],
        out_shape=[
            jax.ShapeDtypeStruct((n_tok, 3 * fox_w), _BF16),
            jax.ShapeDtypeStruct((n_tok, LANES), _F32),
            jax.ShapeDtypeStruct((n_tok, conv_wd), _BF16),
            jax.ShapeDtypeStruct((n_tok, gate_w), _BF16),
        ],
        scratch_shapes=[pltpu.VMEM((2 * SUBLANES, conv_wd), _F32)],
        compiler_params=_params("arbitrary", "arbitrary"),
        name="inproj",
    )(h, gain, w_qkv, w_rest, w_f, bf_pad, qg, kg, conv_w_l)


def _decay_kernel(lf_ref, tri_ref, sq_ref, sk_ref, constq_ref, constk_ref, eq_ref, ek_ref,
                  *, seq, blk, n_heads):
    tri = tri_ref[...]
    lane = lax.broadcasted_iota(jnp.int32, (1, LANES), 1)
    carry = jnp.zeros((1, LANES), _F32)
    for i in range(seq // blk):
        rows = slice(i * blk, (i + 1) * blk)
        hi, mid, lo = _split3(lf_ref[rows, :])
        c = (_dot(tri, lo) + _dot(tri, mid)) + _dot(tri, hi) + carry
        carry = c[blk - 1:blk, :]
        hi, mid, lo = _split3(c * LOG2E)
        parts = jnp.where(lane < n_heads, hi, jnp.where(lane < 2 * n_heads, mid, lo))
        eq_ref[rows, :] = (_dot(parts, sq_ref[...]) + constq_ref[...]).astype(_BF16)
        ek_ref[rows, :] = (_dot(parts, sk_ref[...]) + constk_ref[...]).astype(_BF16)


def _decay(logf, consts, *, batch, seq, n_heads):
    tri, sq = consts[0], consts[1]
    blk = tri.shape[0]
    aug_w = sq.shape[1]
    n_tok = logf.shape[0]
    const = lambda b: (0, 0)
    out = jax.ShapeDtypeStruct((n_tok, aug_w), _BF16)
    return pl.pallas_call(
        functools.partial(_decay_kernel, seq=seq, blk=blk, n_heads=n_heads),
        grid=(batch,),
        in_specs=[pl.BlockSpec((seq, LANES), lambda b: (b, 0))]
        + [_resident(c.shape, const) for c in consts],
        out_specs=[pl.BlockSpec((seq, aug_w), lambda b: (b, 0))] * 2,
        out_shape=[out, out],
        compiler_params=_params("arbitrary"),
        name="decay",
    )(logf, *consts)


def _decay_constants(n_heads, head_dim):
    assert DECAY_PARTS * n_heads <= LANES and 2 * DECAY_PARTS <= head_dim
    r = np.arange(CUMSUM_BLOCK)
    tri = (r[:, None] >= r[None, :]).astype(np.float32)
    width = (n_heads // 2) * LANES
    sq = np.zeros((LANES, width), np.float32)
    sk = np.zeros_like(sq)
    constq = np.zeros((1, width), np.float32)
    constk = np.zeros_like(constq)
    for h in range(n_heads):
        base = (h // 2) * LANES + (head_dim if h % 2 == 0 else 0)
        for part in range(DECAY_PARTS):
            sq[part * n_heads + h, base + part] = 1.0
            sk[part * n_heads + h, base + DECAY_PARTS + part] = -1.0
            constq[0, base + DECAY_PARTS + part] = 1.0
            constk[0, base + part] = 1.0
    return (jnp.asarray(tri, _BF16), jnp.asarray(sq, _BF16), jnp.asarray(sk, _BF16),
            jnp.asarray(constq), jnp.asarray(constk))


def _fox_kernel(q_ref, eq_ref, k_ref, ek_ref, v_ref, o_ref, *, seq, tq, tk, head_dim):
    lane = lax.broadcasted_iota(jnp.int32, (1, 2 * head_dim), 1)
    first = lane < head_dim
    rows = lax.broadcasted_iota(jnp.int32, (tq, tk), 0)
    cols = lax.broadcasted_iota(jnp.int32, (tq, tk), 1)
    causal = rows >= cols
    pair_w = 2 * head_dim
    ones_col = jnp.where(lax.broadcasted_iota(jnp.int32, (tk, LANES), 1) == 0, 1.0, 0.0).astype(_BF16)

    for qi in range(seq // tq):
        qs = slice(qi * tq, (qi + 1) * tq)
        q = q_ref[qs, :]
        eq = eq_ref[qs, :]
        q_aug = (jnp.where(first, q, eq), jnp.where(first, eq, q))
        state = [(jnp.full((tq, 1), MASK_VALUE, _F32), jnp.zeros((tq, pair_w + LANES), _F32))
                 for _ in range(2)]
        for j in range(qi + 1):
            ks = slice(j * tk, (j + 1) * tk)
            kc = k_ref[ks, :]
            v_aug = jnp.concatenate([v_ref[ks, :], ones_col], axis=1)
            ekc = ek_ref[ks, :]
            k_aug = (jnp.where(first, kc, ekc), jnp.where(first, ekc, kc))
            for e in range(2):
                m, acc = state[e]
                s = _dot_nt(q_aug[e], k_aug[e])
                if j == qi:
                    s = jnp.where(causal, s, MASK_VALUE)
                m_new = jnp.maximum(m, jnp.max(s, axis=1, keepdims=True))
                alpha = jnp.exp2(m - m_new)
                p = jnp.exp2(s - m_new).astype(_BF16)
                state[e] = (m_new, alpha * acc + _dot(p, v_aug))
        (_, acc0), (_, acc1) = state
        o_ref[qs, :] = jnp.where(first, acc0[:, :pair_w] / acc0[:, pair_w:pair_w + 1],
                                 acc1[:, :pair_w] / acc1[:, pair_w:pair_w + 1]).astype(_BF16)


def _fox(qkv, eq, ek, *, batch, seq, n_heads, head_dim):
    n_tok = qkv.shape[0]
    fox_w = n_heads * head_dim
    n_pairs = n_heads // 2
    pair_w = 2 * head_dim
    tq, tk = ATTN_Q_TILE, ATTN_K_TILE
    assert tq == tk and pair_w == LANES
    return pl.pallas_call(
        functools.partial(_fox_kernel, seq=seq, tq=tq, tk=tk, head_dim=head_dim),
        grid=(batch, n_pairs),
        in_specs=[
            pl.BlockSpec((seq, pair_w), lambda b, p: (b, p)),
            pl.BlockSpec((seq, LANES), lambda b, p: (b, p)),
            pl.BlockSpec((seq, pair_w), lambda b, p: (b, n_pairs + p)),
            pl.BlockSpec((seq, LANES), lambda b, p: (b, p)),
            pl.BlockSpec((seq, pair_w), lambda b, p: (b, 2 * n_pairs + p)),
        ],
        out_specs=pl.BlockSpec((seq, pair_w), lambda b, p: (b, p)),
        out_shape=jax.ShapeDtypeStruct((n_tok, fox_w), _BF16),
        compiler_params=_params("arbitrary", "arbitrary"),
        name="fox",
    )(qkv, eq, qkv, ek, qkv)


def _post_kernel(a_ref, cc_ref, gate_ref, h_ref, wua_ref, wub_ref, wo_ref, gq_ref, wcq_ref,
                 cqg_ref, km_ref, vm_ref, wco_ref, o_ref, *, n_heads, head_dim):
    d_model = h_ref.shape[-1]
    up_a = _dot(a_ref[...], wua_ref[...])
    up_b = _dot(cc_ref[...], wub_ref[...])
    merged = (gate_ref[:, :d_model].astype(_F32) * up_a
              + gate_ref[:, d_model:].astype(_F32) * up_b)
    h1 = h_ref[...] + _dot(merged.astype(_BF16), wo_ref[...])

    u = _rms_rows(h1, gq_ref[...]).astype(_BF16)
    qm = _dot(u, wcq_ref[...])
    scale = head_dim ** -0.5
    heads = []
    for e in range(n_heads):
        sl = slice(e * head_dim, (e + 1) * head_dim)
        qe = qm[:, sl]
        ms = jnp.mean(qe * qe, axis=-1, keepdims=True)
        qn = (qe * lax.rsqrt(ms + EPS) * cqg_ref[...]).astype(_BF16)
        s = _dot_nt(qn, km_ref[:, sl]) * scale
        p = jnp.exp(s - jnp.max(s, axis=1, keepdims=True))
        den = jnp.sum(p, axis=1, keepdims=True)
        heads.append((_dot(p.astype(_BF16), vm_ref[:, sl]) / den).astype(_BF16))
    o_ref[...] = h1 + _dot(jnp.concatenate(heads, axis=1), wco_ref[...])


def _post(a_out, cc, gates, h, layer, w_up_a, w_up_b, w_o, gq, w_cq, cq_gain_l, kmem, vmem, w_co,
          *, batch, seq):
    n_tok, d_model = h.shape
    tm = TOKEN_TILE
    nt = seq // tm
    fox_w = a_out.shape[1]
    conv_wd = cc.shape[1]
    m_tok, mem_w = kmem.shape[2], kmem.shape[3]
    head_dim = cq_gain_l.shape[-1]
    row = lambda b, j: (b * nt + j, 0)
    const2 = lambda b, j: (0, 0)
    lay3 = lambda b, j: (layer, 0, 0)
    return pl.pallas_call(
        functools.partial(_post_kernel, n_heads=mem_w // head_dim, head_dim=head_dim),
        grid=(batch, nt),
        in_specs=[
            pl.BlockSpec((tm, fox_w), row),
            pl.BlockSpec((tm, conv_wd), row),
            pl.BlockSpec((tm, 2 * d_model), row),
            pl.BlockSpec((tm, d_model), row),
            _resident((None, fox_w, d_model), lay3),
            _resident((None, conv_wd, d_model), lay3),
            _resident((None, d_model, d_model), lay3),
            _resident((1, d_model), const2),
            _resident((None, d_model, mem_w), lay3),
            _resident((1, head_dim), const2),
            pl.BlockSpec((None, None, m_tok, mem_w), lambda b, j: (layer, b, 0, 0)),
            pl.BlockSpec((None, None, m_tok, mem_w), lambda b, j: (layer, b, 0, 0)),
            _resident((None, mem_w, d_model), lay3),
        ],
        out_specs=pl.BlockSpec((tm, d_model), row),
        out_shape=jax.ShapeDtypeStruct((n_tok, d_model), _F32),
        compiler_params=_params("arbitrary", "arbitrary"),
        name="post",
    )(a_out, cc, gates, h, w_up_a, w_up_b, w_o, gq, w_cq, cq_gain_l, kmem, vmem, w_co)


def _ffn_kernel(h_ref, g_ref, wgu_ref, wd_ref, o_ref, act_s, *, d_ff, chunk):
    h = h_ref[...]
    u = _rms_rows(h, g_ref[...]).astype(_BF16)
    for c in range(d_ff // chunk):
        gate = _dot(u, wgu_ref[:, c * chunk:(c + 1) * chunk])
        up = _dot(u, wgu_ref[:, d_ff + c * chunk:d_ff + (c + 1) * chunk])
        act_s[:, c * chunk:(c + 1) * chunk] = (gate * jax.nn.sigmoid(gate) * up).astype(_BF16)
    o_ref[...] = h + _dot(act_s[...], wd_ref[...])


def _ffn(h, layer, gain, w_gu, w_down):
    n_tok, d_model = h.shape
    d_ff = w_down.shape[1]
    tm = TOKEN_TILE
    assert d_ff % FFN_CHUNK == 0
    return pl.pallas_call(
        functools.partial(_ffn_kernel, d_ff=d_ff, chunk=FFN_CHUNK),
        grid=(n_tok // tm,),
        in_specs=[
            pl.BlockSpec((tm, d_model), lambda i: (i, 0)),
            _resident((1, d_model), lambda i: (0, 0)),
            _resident((None, d_model, 2 * d_ff), lambda i: (layer, 0, 0)),
            _resident((None, d_ff, d_model), lambda i: (layer, 0, 0)),
        ],
        out_specs=pl.BlockSpec((tm, d_model), lambda i: (i, 0)),
        out_shape=jax.ShapeDtypeStruct((n_tok, d_model), _F32),
        scratch_shapes=[pltpu.VMEM((tm, d_ff), _BF16)],
        compiler_params=_params("arbitrary"),
        name="ffn",
    )(h, gain, w_gu, w_down)


def kernel(x, mem, norm_mix, w_in, b_f, q_gain, k_gain, conv_w, w_up_a, w_up_b, w_o,
           norm_mem_q, norm_mem_kv, w_cq, w_ckv, cq_gain, ck_gain, w_co, norm_ffn, w_gu, w_down):
    batch, seq, d_model = x.shape
    n_layers = w_in.shape[0]
    head_dim = q_gain.shape[-1]
    n_heads = b_f.shape[-1]
    fox_w = n_heads * head_dim
    conv_wd = conv_w.shape[-1]
    gate_w = 2 * d_model
    assert seq % TOKEN_TILE == 0 and seq % ATTN_Q_TILE == 0 and seq % CUMSUM_BLOCK == 0
    assert w_in.shape[-1] == 3 * fox_w + n_heads + 3 * conv_wd + gate_w

    f0 = 3 * fox_w
    w_qkv = w_in[..., :f0].astype(_BF16)
    w_rest = w_in[..., f0 + n_heads:].astype(_BF16)
    f_pad = LANES - DECAY_PARTS * n_heads
    w_f = jnp.pad(jnp.tile(w_in[..., f0:f0 + n_heads], (1, 1, DECAY_PARTS)),
                  ((0, 0), (0, 0), (0, f_pad))).astype(_BF16)
    bf_pad = jnp.pad(jnp.tile(b_f, (1, DECAY_PARTS)), ((0, 0), (0, f_pad)))
    w_up_a_bf, w_up_b_bf, w_o_bf = (w.astype(_BF16) for w in (w_up_a, w_up_b, w_o))
    w_cq_bf, w_ckv_bf, w_co_bf = (w.astype(_BF16) for w in (w_cq, w_ckv, w_co))
    w_gu_bf, w_down_bf = w_gu.astype(_BF16), w_down.astype(_BF16)
    qg = jnp.tile(q_gain, (1, n_heads)) * (head_dim ** -0.5 * LOG2E)
    kg = jnp.tile(k_gain, (1, n_heads))

    decay_consts = _decay_constants(n_heads, head_dim)

    kmem, vmem = _memkv(mem, norm_mem_kv, w_ckv_bf, ck_gain)

    h = x.reshape(batch * seq, d_model)
    dims = (fox_w, conv_wd, gate_w, head_dim)
    for l in range(n_layers):
        qkv, logf, cc, gates = _inproj(h, l, norm_mix[l][None], w_qkv, w_rest, w_f, bf_pad[l][None],
                                       qg[l][None], kg[l][None], conv_w[l],
                                       batch=batch, seq=seq, dims=dims)
        eq, ek = _decay(logf, decay_consts, batch=batch, seq=seq, n_heads=n_heads)
        a_out = _fox(qkv, eq, ek, batch=batch, seq=seq, n_heads=n_heads, head_dim=head_dim)
        h = _post(a_out, cc, gates, h, l, w_up_a_bf, w_up_b_bf, w_o_bf, norm_mem_q[l][None],
                  w_cq_bf, cq_gain[l][None], kmem, vmem, w_co_bf, batch=batch, seq=seq)
        h = _ffn(h, l, norm_ffn[l][None], w_gu_bf, w_down_bf)
    return h.reshape(batch, seq, d_model)
```

```python
import functools

import jax
import jax.numpy as jnp
import numpy as np
from jax import lax
from jax.experimental import pallas as pl
from jax.experimental.pallas import tpu as pltpu

_BF16 = jnp.bfloat16
_F32 = jnp.float32

EPS = 1e-6
LOG2E = 1.4426950408889634
LANES = 128
SUBLANES = 8
VMEM_LIMIT_BYTES = 56 * 1024 * 1024
MASK_VALUE = -1e30
TOKEN_TILE = 1024
MERGE_CHUNK = 256
ATTN_Q_TILE = 512
ATTN_K_TILE = 512
CUMSUM_BLOCK = 256
DECAY_PARTS = 3
FFN_CHUNK = 256


def _dot(a, b):
    return jnp.dot(a, b, preferred_element_type=_F32)


def _dot_nt(a, b):
    return lax.dot_general(a, b, (((1,), (1,)), ((), ())), preferred_element_type=_F32)


def _rms_rows(x, gain):
    ms = jnp.mean(x * x, axis=-1, keepdims=True)
    return x * lax.rsqrt(ms + EPS) * gain


def _split3(x):
    hi = x.astype(_BF16)
    r1 = x - hi.astype(_F32)
    mid = r1.astype(_BF16)
    lo = (r1 - mid.astype(_F32)).astype(_BF16)
    return hi, mid, lo


def _params(*sem):
    return pltpu.CompilerParams(dimension_semantics=sem, vmem_limit_bytes=VMEM_LIMIT_BYTES)


def _resident(block_shape, index_map):
    return pl.BlockSpec(block_shape, index_map, pipeline_mode=pl.Buffered(1))


def _memkv_kernel(mem_ref, g_ref, w_ref, ckg_ref, k_ref, v_ref, *, n_heads, head_dim):
    u = _rms_rows(mem_ref[...], g_ref[...]).astype(_BF16)
    kv = _dot(u, w_ref[...])
    width = n_heads * head_dim
    for e in range(n_heads):
        ke = kv[:, e * head_dim:(e + 1) * head_dim]
        ms = jnp.mean(ke * ke, axis=-1, keepdims=True)
        k_ref[:, e * head_dim:(e + 1) * head_dim] = (
            ke * lax.rsqrt(ms + EPS) * ckg_ref[...]).astype(_BF16)
    v_ref[...] = kv[:, width:].astype(_BF16)


def _memkv(mem, norm_mem_kv, w_ckv_bf, ck_gain):
    n_layers, d_model, two_w = w_ckv_bf.shape
    batch, m_tok, _ = mem.shape
    width = two_w // 2
    head_dim = ck_gain.shape[-1]
    n_heads = width // head_dim
    out = jax.ShapeDtypeStruct((n_layers, batch, m_tok, width), _BF16)
    return pl.pallas_call(
        functools.partial(_memkv_kernel, n_heads=n_heads, head_dim=head_dim),
        grid=(n_layers, batch),
        in_specs=[
            pl.BlockSpec((None, m_tok, d_model), lambda l, b: (b, 0, 0)),
            pl.BlockSpec((None, 1, d_model), lambda l, b: (l, 0, 0)),
            pl.BlockSpec((None, d_model, two_w), lambda l, b: (l, 0, 0)),
            pl.BlockSpec((None, 1, head_dim), lambda l, b: (l, 0, 0)),
        ],
        out_specs=[
            pl.BlockSpec((None, None, m_tok, width), lambda l, b: (l, b, 0, 0)),
            pl.BlockSpec((None, None, m_tok, width), lambda l, b: (l, b, 0, 0)),
        ],
        out_shape=[out, out],
        compiler_params=_params("arbitrary", "arbitrary"),
        name="memkv",
    )(mem, norm_mem_kv[:, None, :], w_ckv_bf, ck_gain[:, None, :])


def _inproj_kernel(h_ref, g_ref, wqkv_ref, wrest_ref, wf_ref, bf_ref, qg_ref, kg_ref, cw_ref,
                   qkv_ref, lf_ref, cc_ref, gate_ref, tail_s,
                   *, tm, fox_w, conv_w, gate_w, head_dim):
    j = pl.program_id(1)
    u = _rms_rows(h_ref[...], g_ref[...]).astype(_BF16)

    assert 2 * head_dim == LANES
    first = lax.broadcasted_iota(jnp.int32, (1, LANES), 1) < head_dim
    for idx, gain_ref in ((0, qg_ref), (1, kg_ref)):
        a = _dot(u, wqkv_ref[:, idx * fox_w:(idx + 1) * fox_w])
        for c in range(fox_w // LANES):
            cs = slice(c * LANES, (c + 1) * LANES)
            ac = a[:, cs]
            sq = ac * ac
            lo = jnp.sum(jnp.where(first, sq, 0.0), axis=1, keepdims=True)
            hi = jnp.sum(jnp.where(first, 0.0, sq), axis=1, keepdims=True)
            ss = jnp.where(first, lo, hi)
            qkv_ref[:, idx * fox_w + cs.start:idx * fox_w + cs.stop] = (
                ac * lax.rsqrt(ss * (1.0 / head_dim) + EPS) * gain_ref[:, cs]).astype(_BF16)
    qkv_ref[:, 2 * fox_w:3 * fox_w] = _dot(u, wqkv_ref[:, 2 * fox_w:3 * fox_w]).astype(_BF16)

    @pl.when(j == 0)
    def _():
        tail_s[...] = jnp.zeros(tail_s.shape, _F32)

    n_taps = cw_ref.shape[0]
    hr = tail_s.shape[0]
    assert n_taps - 1 <= hr and conv_w % (2 * LANES) == 0
    sub = lax.broadcasted_iota(jnp.int32, (hr, LANES), 0)
    for cp in range(conv_w // (2 * LANES)):
        gate_b = _dot(u, wrest_ref[:, conv_w + cp * 2 * LANES:conv_w + (cp + 1) * 2 * LANES])
        for half in range(2):
            cs = slice((2 * cp + half) * LANES, (2 * cp + half + 1) * LANES)
            w_pair = jnp.concatenate(
                [wrest_ref[:, cs], wrest_ref[:, 2 * conv_w + cs.start:2 * conv_w + cs.stop]], axis=1)
            r = _dot(u, w_pair)
            zc = r[:, :LANES] * r[:, LANES:]
            y = cw_ref[n_taps - 1:n_taps, cs] * zc
            head = zc[0:hr, :]
            y_head = cw_ref[n_taps - 1:n_taps, cs] * head
            prev = tail_s[:, cs]
            for d in range(1, n_taps):
                w_d = cw_ref[n_taps - 1 - d:n_taps - d, cs]
                y = y + w_d * pltpu.roll(zc, d, axis=0)
                y_head = y_head + w_d * jnp.where(sub < d, pltpu.roll(prev, d, axis=0),
                                                  pltpu.roll(head, d, axis=0))
            gb = gate_b[:, half * LANES:(half + 1) * LANES]
            cc_ref[0:hr, cs] = (gb[0:hr, :] * y_head).astype(_BF16)
            cc_ref[hr:tm, cs] = (gb[hr:tm, :] * y[hr:tm, :]).astype(_BF16)
            tail_s[:, cs] = zc[tm - hr:tm, :]

    o = 3 * conv_w
    step = conv_w
    for i in range(gate_w // step):
        a = _dot(u, wrest_ref[:, o + i * step:o + (i + 1) * step])
        gate_ref[:, i * step:(i + 1) * step] = jax.nn.sigmoid(a).astype(_BF16)

    a = _dot(u, wf_ref[...]) + bf_ref[...]
    lf_ref[...] = jnp.minimum(a, 0.0) - jnp.log1p(jnp.exp(-jnp.abs(a)))


def _inproj(h, layer, gain, w_qkv, w_rest, w_f, bf_pad, qg, kg, conv_w_l, *, batch, seq, dims):
    fox_w, conv_wd, gate_w, head_dim = dims
    n_tok, d_model = h.shape
    tm = TOKEN_TILE
    nt = seq // tm
    row = lambda b, j: (b * nt + j, 0)
    const2 = lambda b, j: (0, 0)
    lay3 = lambda b, j: (layer, 0, 0)
    return pl.pallas_call(
        functools.partial(_inproj_kernel, tm=tm, fox_w=fox_w, conv_w=conv_wd, gate_w=gate_w,
                          head_dim=head_dim),
        grid=(batch, nt),
        in_specs=[
            pl.BlockSpec((tm, d_model), row),
            _resident((1, d_model), const2),
            _resident((None, d_model, w_qkv.shape[-1]), lay3),
            _resident((None, d_model, w_rest.shape[-1]), lay3),
            _resident((None, d_model, LANES), lay3),
            _resident((1, LANES), const2),
            _resident((1, fox_w), const2),
            _resident((1, fox_w), const2),
            _resident(conv_w_l.shape, const2),
        ],
        out_specs=[
            pl.BlockSpec((tm, 3 * fox_w), row),
            pl.BlockSpec((tm, LANES), row),
            pl.BlockSpec((tm, conv_wd), row),
            pl.BlockSpec((tm, gate_w), row),
        ],
        out_shape=[
            jax.ShapeDtypeStruct((n_tok, 3 * fox_w), _BF16),
            jax.ShapeDtypeStruct((n_tok, LANES), _F32),
            jax.ShapeDtypeStruct((n_tok, conv_wd), _BF16),
            jax.ShapeDtypeStruct((n_tok, gate_w), _BF16),
        ],
        scratch_shapes=[pltpu.VMEM((2 * SUBLANES, conv_wd), _F32)],
        compiler_params=_params("arbitrary", "arbitrary"),
        name="inproj",
    )(h, gain, w_qkv, w_rest, w_f, bf_pad, qg, kg, conv_w_l)


def _decay_kernel(lf_ref, tri_ref, sq_ref, sk_ref, constq_ref, constk_ref, eq_ref, ek_ref,
                  *, seq, blk, n_heads):
    tri = tri_ref[...]
    lane = lax.broadcasted_iota(jnp.int32, (1, LANES), 1)
    carry = jnp.zeros((1, LANES), _F32)
    for i in range(seq // blk):
        rows = slice(i * blk, (i + 1) * blk)
        hi, mid, lo = _split3(lf_ref[rows, :])
        c = (_dot(tri, lo) + _dot(tri, mid)) + _dot(tri, hi) + carry
        carry = c[blk - 1:blk, :]
        hi, mid, lo = _split3(c * LOG2E)
        parts = jnp.where(lane < n_heads, hi, jnp.where(lane < 2 * n_heads, mid, lo))
        eq_ref[rows, :] = (_dot(parts, sq_ref[...]) + constq_ref[...]).astype(_BF16)
        ek_ref[rows, :] = (_dot(parts, sk_ref[...]) + constk_ref[...]).astype(_BF16)


def _decay(logf, consts, *, batch, seq, n_heads):
    tri, sq = consts[0], consts[1]
    blk = tri.shape[0]
    aug_w = sq.shape[1]
    n_tok = logf.shape[0]
    const = lambda b: (0, 0)
    out = jax.ShapeDtypeStruct((n_tok, aug_w), _BF16)
    return pl.pallas_call(
        functools.partial(_decay_kernel, seq=seq, blk=blk, n_heads=n_heads),
        grid=(batch,),
        in_specs=[pl.BlockSpec((seq, LANES), lambda b: (b, 0))]
        + [_resident(c.shape, const) for c in consts],
        out_specs=[pl.BlockSpec((seq, aug_w), lambda b: (b, 0))] * 2,
        out_shape=[out, out],
        compiler_params=_params("arbitrary"),
        name="decay",
    )(logf, *consts)


def _decay_constants(n_heads, head_dim):
    assert DECAY_PARTS * n_heads <= LANES and 2 * DECAY_PARTS <= head_dim
    r = np.arange(CUMSUM_BLOCK)
    tri = (r[:, None] >= r[None, :]).astype(np.float32)
    width = (n_heads // 2) * LANES
    sq = np.zeros((LANES, width), np.float32)
    sk = np.zeros_like(sq)
    constq = np.zeros((1, width), np.float32)
    constk = np.zeros_like(constq)
    for h in range(n_heads):
        base = (h // 2) * LANES + (head_dim if h % 2 == 0 else 0)
        for part in range(DECAY_PARTS):
            sq[part * n_heads + h, base + part] = 1.0
            sk[part * n_heads + h, base + DECAY_PARTS + part] = -1.0
            constq[0, base + DECAY_PARTS + part] = 1.0
            constk[0, base + part] = 1.0
    return (jnp.asarray(tri, _BF16), jnp.asarray(sq, _BF16), jnp.asarray(sk, _BF16),
            jnp.asarray(constq), jnp.asarray(constk))


def _fox_kernel(q_ref, eq_ref, k_ref, ek_ref, v_ref, o_ref, *, seq, tq, tk, head_dim):
    lane = lax.broadcasted_iota(jnp.int32, (1, 2 * head_dim), 1)
    first = lane < head_dim
    rows = lax.broadcasted_iota(jnp.int32, (tq, tk), 0)
    cols = lax.broadcasted_iota(jnp.int32, (tq, tk), 1)
    causal = rows >= cols
    pair_w = 2 * head_dim
    ones_col = jnp.where(lax.broadcasted_iota(jnp.int32, (tk, LANES), 1) == 0, 1.0, 0.0).astype(_BF16)

    for qi in range(seq // tq):
        qs = slice(qi * tq, (qi + 1) * tq)
        q = q_ref[qs, :]
        eq = eq_ref[qs, :]
        q_aug = (jnp.where(first, q, eq), jnp.where(first, eq, q))
        state = [(jnp.full((tq, 1), MASK_VALUE, _F32), jnp.zeros((tq, pair_w + LANES), _F32))
                 for _ in range(2)]
        for j in range(qi + 1):
            ks = slice(j * tk, (j + 1) * tk)
            kc = k_ref[ks, :]
            v_aug = jnp.concatenate([v_ref[ks, :], ones_col], axis=1)
            ekc = ek_ref[ks, :]
            k_aug = (jnp.where(first, kc, ekc), jnp.where(first, ekc, kc))
            for e in range(2):
                m, acc = state[e]
                s = _dot_nt(q_aug[e], k_aug[e])
                if j == qi:
                    s = jnp.where(causal, s, MASK_VALUE)
                m_new = jnp.maximum(m, jnp.max(s, axis=1, keepdims=True))
                alpha = jnp.exp2(m - m_new)
                p = jnp.exp2(s - m_new).astype(_BF16)
                state[e] = (m_new, alpha * acc + _dot(p, v_aug))
        (_, acc0), (_, acc1) = state
        o_ref[qs, :] = jnp.where(first, acc0[:, :pair_w] / acc0[:, pair_w:pair_w + 1],
                                 acc1[:, :pair_w] / acc1[:, pair_w:pair_w + 1]).astype(_BF16)


def _fox(qkv, eq, ek, *, batch, seq, n_heads, head_dim):
    n_tok = qkv.shape[0]
    fox_w = n_heads * head_dim
    n_pairs = n_heads // 2
    pair_w = 2 * head_dim
    tq, tk = ATTN_Q_TILE, ATTN_K_TILE
    assert tq == tk and pair_w == LANES
    return pl.pallas_call(
        functools.partial(_fox_kernel, seq=seq, tq=tq, tk=tk, head_dim=head_dim),
        grid=(batch, n_pairs),
        in_specs=[
            pl.BlockSpec((seq, pair_w), lambda b, p: (b, p)),
            pl.BlockSpec((seq, LANES), lambda b, p: (b, p)),
            pl.BlockSpec((seq, pair_w), lambda b, p: (b, n_pairs + p)),
            pl.BlockSpec((seq, LANES), lambda b, p: (b, p)),
            pl.BlockSpec((seq, pair_w), lambda b, p: (b, 2 * n_pairs + p)),
        ],
        out_specs=pl.BlockSpec((seq, pair_w), lambda b, p: (b, p)),
        out_shape=jax.ShapeDtypeStruct((n_tok, fox_w), _BF16),
        compiler_params=_params("arbitrary", "arbitrary"),
        name="fox",
    )(qkv, eq, qkv, ek, qkv)


def _post_kernel(a_ref, cc_ref, gate_ref, h_ref, wua_ref, wub_ref, wo_ref, gq_ref, wcq_ref,
                 cqg_ref, km_ref, vm_ref, wco_ref, o_ref, merged_s, *, n_heads, head_dim):
    d_model = h_ref.shape[-1]
    scale = head_dim ** -0.5
    a = a_ref[...]
    cc = cc_ref[...]
    for c in range(d_model // MERGE_CHUNK):
        cs = slice(c * MERGE_CHUNK, (c + 1) * MERGE_CHUNK)
        up_a = _dot(a, wua_ref[:, cs])
        up_b = _dot(cc, wub_ref[:, cs])
        merged_s[:, cs] = (gate_ref[:, cs].astype(_F32) * up_a
                           + gate_ref[:, d_model + cs.start:d_model + cs.stop].astype(_F32) * up_b
                           ).astype(_BF16)
    h1 = h_ref[...] + _dot(merged_s[...], wo_ref[...])

    u = _rms_rows(h1, gq_ref[...]).astype(_BF16)
    qm = _dot(u, wcq_ref[...])
    heads = []
    for e in range(n_heads):
        sl = slice(e * head_dim, (e + 1) * head_dim)
        qe = qm[:, sl]
        ms = jnp.mean(qe * qe, axis=-1, keepdims=True)
        qn = (qe * lax.rsqrt(ms + EPS) * cqg_ref[...]).astype(_BF16)
        s = _dot_nt(qn, km_ref[:, sl]) * scale
        p = jnp.exp(s - jnp.max(s, axis=1, keepdims=True))
        den = jnp.sum(p, axis=1, keepdims=True)
        heads.append((_dot(p.astype(_BF16), vm_ref[:, sl]) / den).astype(_BF16))
    o_ref[...] = h1 + _dot(jnp.concatenate(heads, axis=1), wco_ref[...])


def _post(a_out, cc, gates, h, layer, w_up_a, w_up_b, w_o, gq, w_cq, cq_gain_l, kmem, vmem, w_co,
          *, batch, seq):
    n_tok, d_model = h.shape
    tm = TOKEN_TILE
    nt = seq // tm
    fox_w = a_out.shape[1]
    conv_wd = cc.shape[1]
    m_tok, mem_w = kmem.shape[2], kmem.shape[3]
    head_dim = cq_gain_l.shape[-1]
    row = lambda b, j: (b * nt + j, 0)
    const2 = lambda b, j: (0, 0)
    lay3 = lambda b, j: (layer, 0, 0)
    return pl.pallas_call(
        functools.partial(_post_kernel, n_heads=mem_w // head_dim, head_dim=head_dim),
        grid=(batch, nt),
        in_specs=[
            pl.BlockSpec((tm, fox_w), row),
            pl.BlockSpec((tm, conv_wd), row),
            pl.BlockSpec((tm, 2 * d_model), row),
            pl.BlockSpec((tm, d_model), row),
            _resident((None, fox_w, d_model), lay3),
            _resident((None, conv_wd, d_model), lay3),
            _resident((None, d_model, d_model), lay3),
            _resident((1, d_model), const2),
            _resident((None, d_model, mem_w), lay3),
            _resident((1, head_dim), const2),
            pl.BlockSpec((None, None, m_tok, mem_w), lambda b, j: (layer, b, 0, 0)),
            pl.BlockSpec((None, None, m_tok, mem_w), lambda b, j: (layer, b, 0, 0)),
            _resident((None, mem_w, d_model), lay3),
        ],
        out_specs=pl.BlockSpec((tm, d_model), row),
        out_shape=jax.ShapeDtypeStruct((n_tok, d_model), _F32),
        scratch_shapes=[pltpu.VMEM((tm, d_model), _BF16)],
        compiler_params=_params("arbitrary", "arbitrary"),
        name="post",
    )(a_out, cc, gates, h, w_up_a, w_up_b, w_o, gq, w_cq, cq_gain_l, kmem, vmem, w_co)


def _ffn_kernel(h_ref, g_ref, wgu_ref, wd_ref, o_ref, act_s, *, d_ff, chunk):
    h = h_ref[...]
    u = _rms_rows(h, g_ref[...]).astype(_BF16)
    for c in range(d_ff // chunk):
        gate = _dot(u, wgu_ref[:, c * chunk:(c + 1) * chunk])
        up = _dot(u, wgu_ref[:, d_ff + c * chunk:d_ff + (c + 1) * chunk])
        act_s[:, c * chunk:(c + 1) * chunk] = (gate * jax.nn.sigmoid(gate) * up).astype(_BF16)
    o_ref[...] = h + _dot(act_s[...], wd_ref[...])


def _ffn(h, layer, gain, w_gu, w_down):
    n_tok, d_model = h.shape
    d_ff = w_down.shape[1]
    tm = TOKEN_TILE
    assert d_ff % FFN_CHUNK == 0
    return pl.pallas_call(
        functools.partial(_ffn_kernel, d_ff=d_ff, chunk=FFN_CHUNK),
        grid=(n_tok // tm,),
        in_specs=[
            pl.BlockSpec((tm, d_model), lambda i: (i, 0)),
            _resident((1, d_model), lambda i: (0, 0)),
            _resident((None, d_model, 2 * d_ff), lambda i: (layer, 0, 0)),
            _resident((None, d_ff, d_model), lambda i: (layer, 0, 0)),
        ],
        out_specs=pl.BlockSpec((tm, d_model), lambda i: (i, 0)),
        out_shape=jax.ShapeDtypeStruct((n_tok, d_model), _F32),
        scratch_shapes=[pltpu.VMEM((tm, d_ff), _BF16)],
        compiler_params=_params("arbitrary"),
        name="ffn",
    )(h, gain, w_gu, w_down)


def kernel(x, mem, norm_mix, w_in, b_f, q_gain, k_gain, conv_w, w_up_a, w_up_b, w_o,
           norm_mem_q, norm_mem_kv, w_cq, w_ckv, cq_gain, ck_gain, w_co, norm_ffn, w_gu, w_down):
    batch, seq, d_model = x.shape
    n_layers = w_in.shape[0]
    head_dim = q_gain.shape[-1]
    n_heads = b_f.shape[-1]
    fox_w = n_heads * head_dim
    conv_wd = conv_w.shape[-1]
    gate_w = 2 * d_model
    assert seq % TOKEN_TILE == 0 and seq % ATTN_Q_TILE == 0 and seq % CUMSUM_BLOCK == 0
    assert w_in.shape[-1] == 3 * fox_w + n_heads + 3 * conv_wd + gate_w

    f0 = 3 * fox_w
    w_qkv = w_in[..., :f0].astype(_BF16)
    w_rest = w_in[..., f0 + n_heads:].astype(_BF16)
    f_pad = LANES - DECAY_PARTS * n_heads
    w_f = jnp.pad(jnp.tile(w_in[..., f0:f0 + n_heads], (1, 1, DECAY_PARTS)),
                  ((0, 0), (0, 0), (0, f_pad))).astype(_BF16)
    bf_pad = jnp.pad(jnp.tile(b_f, (1, DECAY_PARTS)), ((0, 0), (0, f_pad)))
    w_up_a_bf, w_up_b_bf, w_o_bf = (w.astype(_BF16) for w in (w_up_a, w_up_b, w_o))
    w_cq_bf, w_ckv_bf, w_co_bf = (w.astype(_BF16) for w in (w_cq, w_ckv, w_co))
    w_gu_bf, w_down_bf = w_gu.astype(_BF16), w_down.astype(_BF16)
    qg = jnp.tile(q_gain, (1, n_heads)) * (head_dim ** -0.5 * LOG2E)
    kg = jnp.tile(k_gain, (1, n_heads))

    decay_consts = _decay_constants(n_heads, head_dim)

    kmem, vmem = _memkv(mem, norm_mem_kv, w_ckv_bf, ck_gain)

    h = x.reshape(batch * seq, d_model)
    dims = (fox_w, conv_wd, gate_w, head_dim)
    for l in range(n_layers):
        qkv, logf, cc, gates = _inproj(h, l, norm_mix[l][None], w_qkv, w_rest, w_f, bf_pad[l][None],
                                       qg[l][None], kg[l][None], conv_w[l],
                                       batch=batch, seq=seq, dims=dims)
        eq, ek = _decay(logf, decay_consts, batch=batch, seq=seq, n_heads=n_heads)
        a_out = _fox(qkv, eq, ek, batch=batch, seq=seq, n_heads=n_heads, head_dim=head_dim)
        h = _post(a_out, cc, gates, h, l, w_up_a_bf, w_up_b_bf, w_o_bf, norm_mem_q[l][None],
                  w_cq_bf, cq_gain[l][None], kmem, vmem, w_co_bf, batch=batch, seq=seq)
        h = _ffn(h, l, norm_ffn[l][None], w_gu_bf, w_down_bf)
    return h.reshape(batch, seq, d_model)
```

```python
import functools

import jax
import jax.numpy as jnp
import numpy as np
from jax import lax
from jax.experimental import pallas as pl
from jax.experimental.pallas import tpu as pltpu

_BF16 = jnp.bfloat16
_F32 = jnp.float32

EPS = 1e-6
LOG2E = 1.4426950408889634
LANES = 128
SUBLANES = 8
VMEM_LIMIT_BYTES = 56 * 1024 * 1024
MASK_VALUE = -1e30
TOKEN_TILE = 1024
MERGE_CHUNK = 256
ATTN_Q_TILE = 512
ATTN_K_TILE = 512
CUMSUM_BLOCK = 256
DECAY_PARTS = 3
FFN_CHUNK = 256


def _dot(a, b):
    return jnp.dot(a, b, preferred_element_type=_F32)


def _dot_nt(a, b):
    return lax.dot_general(a, b, (((1,), (1,)), ((), ())), preferred_element_type=_F32)


def _rms_rows(x, gain):
    ms = jnp.mean(x * x, axis=-1, keepdims=True)
    return x * lax.rsqrt(ms + EPS) * gain


def _split3(x):
    hi = x.astype(_BF16)
    r1 = x - hi.astype(_F32)
    mid = r1.astype(_BF16)
    lo = (r1 - mid.astype(_F32)).astype(_BF16)
    return hi, mid, lo


def _params(*sem):
    return pltpu.CompilerParams(dimension_semantics=sem, vmem_limit_bytes=VMEM_LIMIT_BYTES)


def _resident(block_shape, index_map):
    return pl.BlockSpec(block_shape, index_map, pipeline_mode=pl.Buffered(1))


def _memkv_kernel(mem_ref, g_ref, w_ref, ckg_ref, k_ref, v_ref, *, n_heads, head_dim):
    u = _rms_rows(mem_ref[...], g_ref[...]).astype(_BF16)
    kv = _dot(u, w_ref[...])
    width = n_heads * head_dim
    for e in range(n_heads):
        ke = kv[:, e * head_dim:(e + 1) * head_dim]
        ms = jnp.mean(ke * ke, axis=-1, keepdims=True)
        k_ref[:, e * head_dim:(e + 1) * head_dim] = (
            ke * lax.rsqrt(ms + EPS) * ckg_ref[...]).astype(_BF16)
    v_ref[...] = kv[:, width:].astype(_BF16)


def _memkv(mem, norm_mem_kv, w_ckv_bf, ck_gain):
    n_layers, d_model, two_w = w_ckv_bf.shape
    batch, m_tok, _ = mem.shape
    rows = batch * m_tok
    width = two_w // 2
    head_dim = ck_gain.shape[-1]
    n_heads = width // head_dim
    out = jax.ShapeDtypeStruct((n_layers, rows, width), _BF16)
    return pl.pallas_call(
        functools.partial(_memkv_kernel, n_heads=n_heads, head_dim=head_dim),
        grid=(n_layers,),
        in_specs=[
            _resident((rows, d_model), lambda l: (0, 0)),
            pl.BlockSpec((None, 1, d_model), lambda l: (l, 0, 0)),
            pl.BlockSpec((None, d_model, two_w), lambda l: (l, 0, 0)),
            pl.BlockSpec((None, 1, head_dim), lambda l: (l, 0, 0)),
        ],
        out_specs=[
            pl.BlockSpec((None, rows, width), lambda l: (l, 0, 0)),
            pl.BlockSpec((None, rows, width), lambda l: (l, 0, 0)),
        ],
        out_shape=[out, out],
        compiler_params=_params("arbitrary"),
        name="memkv",
    )(mem.reshape(rows, d_model), norm_mem_kv[:, None, :], w_ckv_bf, ck_gain[:, None, :])


def _inproj_kernel(h_ref, g_ref, wqkv_ref, wrest_ref, wf_ref, bf_ref, qg_ref, kg_ref, cw_ref,
                   qkv_ref, lf_ref, cc_ref, gate_ref, tail_s,
                   *, tm, fox_w, conv_w, gate_w, head_dim):
    j = pl.program_id(1)
    u = _rms_rows(h_ref[...], g_ref[...]).astype(_BF16)

    assert 2 * head_dim == LANES
    first = lax.broadcasted_iota(jnp.int32, (1, LANES), 1) < head_dim
    for idx, gain_ref in ((0, qg_ref), (1, kg_ref)):
        a = _dot(u, wqkv_ref[:, idx * fox_w:(idx + 1) * fox_w])
        for c in range(fox_w // LANES):
            cs = slice(c * LANES, (c + 1) * LANES)
            ac = a[:, cs]
            sq = ac * ac
            lo = jnp.sum(jnp.where(first, sq, 0.0), axis=1, keepdims=True)
            hi = jnp.sum(jnp.where(first, 0.0, sq), axis=1, keepdims=True)
            ss = jnp.where(first, lo, hi)
            qkv_ref[:, idx * fox_w + cs.start:idx * fox_w + cs.stop] = (
                ac * lax.rsqrt(ss * (1.0 / head_dim) + EPS) * gain_ref[:, cs]).astype(_BF16)
    qkv_ref[:, 2 * fox_w:3 * fox_w] = _dot(u, wqkv_ref[:, 2 * fox_w:3 * fox_w]).astype(_BF16)

    @pl.when(j == 0)
    def _():
        tail_s[...] = jnp.zeros(tail_s.shape, _F32)

    n_taps = cw_ref.shape[0]
    hr = tail_s.shape[0]
    assert n_taps - 1 <= hr and conv_w % (2 * LANES) == 0
    sub = lax.broadcasted_iota(jnp.int32, (hr, LANES), 0)
    for cp in range(conv_w // (2 * LANES)):
        gate_b = _dot(u, wrest_ref[:, conv_w + cp * 2 * LANES:conv_w + (cp + 1) * 2 * LANES])
        for half in range(2):
            cs = slice((2 * cp + half) * LANES, (2 * cp + half + 1) * LANES)
            w_pair = jnp.concatenate(
                [wrest_ref[:, cs], wrest_ref[:, 2 * conv_w + cs.start:2 * conv_w + cs.stop]], axis=1)
            r = _dot(u, w_pair)
            zc = r[:, :LANES] * r[:, LANES:]
            y = cw_ref[n_taps - 1:n_taps, cs] * zc
            head = zc[0:hr, :]
            y_head = cw_ref[n_taps - 1:n_taps, cs] * head
            prev = tail_s[:, cs]
            for d in range(1, n_taps):
                w_d = cw_ref[n_taps - 1 - d:n_taps - d, cs]
                y = y + w_d * pltpu.roll(zc, d, axis=0)
                y_head = y_head + w_d * jnp.where(sub < d, pltpu.roll(prev, d, axis=0),
                                                  pltpu.roll(head, d, axis=0))
            gb = gate_b[:, half * LANES:(half + 1) * LANES]
            cc_ref[0:hr, cs] = (gb[0:hr, :] * y_head).astype(_BF16)
            cc_ref[hr:tm, cs] = (gb[hr:tm, :] * y[hr:tm, :]).astype(_BF16)
            tail_s[:, cs] = zc[tm - hr:tm, :]

    o = 3 * conv_w
    step = conv_w
    for i in range(gate_w // step):
        a = _dot(u, wrest_ref[:, o + i * step:o + (i + 1) * step])
        gate_ref[:, i * step:(i + 1) * step] = jax.nn.sigmoid(a).astype(_BF16)

    a = _dot(u, wf_ref[...]) + bf_ref[...]
    lf_ref[...] = jnp.minimum(a, 0.0) - jnp.log1p(jnp.exp(-jnp.abs(a)))


def _inproj(h, layer, gain, w_qkv, w_rest, w_f, bf_pad, qg, kg, conv_w_l, *, batch, seq, dims):
    fox_w, conv_wd, gate_w, head_dim = dims
    n_tok, d_model = h.shape
    tm = TOKEN_TILE
    nt = seq // tm
    row = lambda b, j: (b * nt + j, 0)
    const2 = lambda b, j: (0, 0)
    lay3 = lambda b, j: (layer, 0, 0)
    return pl.pallas_call(
        functools.partial(_inproj_kernel, tm=tm, fox_w=fox_w, conv_w=conv_wd, gate_w=gate_w,
                          head_dim=head_dim),
        grid=(batch, nt),
        in_specs=[
            pl.BlockSpec((tm, d_model), row),
            _resident((1, d_model), const2),
            _resident((None, d_model, w_qkv.shape[-1]), lay3),
            _resident((None, d_model, w_rest.shape[-1]), lay3),
            _resident((None, d_model, LANES), lay3),
            _resident((1, LANES), const2),
            _resident((1, fox_w), const2),
            _resident((1, fox_w), const2),
            _resident(conv_w_l.shape, const2),
        ],
        out_specs=[
            pl.BlockSpec((tm, 3 * fox_w), row),
            pl.BlockSpec((tm, LANES), row),
            pl.BlockSpec((tm, conv_wd), row),
            pl.BlockSpec((tm, gate_w), row),
        ],
        out_shape=[
            jax.ShapeDtypeStruct((n_tok, 3 * fox_w), _BF16),
            jax.ShapeDtypeStruct((n_tok, LANES), _F32),
            jax.ShapeDtypeStruct((n_tok, conv_wd), _BF16),
            jax.ShapeDtypeStruct((n_tok, gate_w), _BF16),
        ],
        scratch_shapes=[pltpu.VMEM((2 * SUBLANES, conv_wd), _F32)],
        compiler_params=_params("arbitrary", "arbitrary"),
        name="inproj",
    )(h, gain, w_qkv, w_rest, w_f, bf_pad, qg, kg, conv_w_l)


def _decay_kernel(lf_ref, tri_ref, sq_ref, sk_ref, constq_ref, constk_ref, eq_ref, ek_ref,
                  *, seq, blk, n_heads):
    tri = tri_ref[...]
    lane = lax.broadcasted_iota(jnp.int32, (1, LANES), 1)
    carry = jnp.zeros((1, LANES), _F32)
    for i in range(seq // blk):
        rows = slice(i * blk, (i + 1) * blk)
        hi, mid, lo = _split3(lf_ref[rows, :])
        c = (_dot(tri, lo) + _dot(tri, mid)) + _dot(tri, hi) + carry
        carry = c[blk - 1:blk, :]
        hi, mid, lo = _split3(c * LOG2E)
        parts = jnp.where(lane < n_heads, hi, jnp.where(lane < 2 * n_heads, mid, lo))
        eq_ref[rows, :] = (_dot(parts, sq_ref[...]) + constq_ref[...]).astype(_BF16)
        ek_ref[rows, :] = (_dot(parts, sk_ref[...]) + constk_ref[...]).astype(_BF16)


def _decay(logf, consts, *, batch, seq, n_heads):
    tri, sq = consts[0], consts[1]
    blk = tri.shape[0]
    aug_w = sq.shape[1]
    n_tok = logf.shape[0]
    const = lambda b: (0, 0)
    out = jax.ShapeDtypeStruct((n_tok, aug_w), _BF16)
    return pl.pallas_call(
        functools.partial(_decay_kernel, seq=seq, blk=blk, n_heads=n_heads),
        grid=(batch,),
        in_specs=[pl.BlockSpec((seq, LANES), lambda b: (b, 0))]
        + [_resident(c.shape, const) for c in consts],
        out_specs=[pl.BlockSpec((seq, aug_w), lambda b: (b, 0))] * 2,
        out_shape=[out, out],
        compiler_params=_params("arbitrary"),
        name="decay",
    )(logf, *consts)


def _decay_constants(n_heads, head_dim):
    assert DECAY_PARTS * n_heads <= LANES and 2 * DECAY_PARTS <= head_dim
    r = np.arange(CUMSUM_BLOCK)
    tri = (r[:, None] >= r[None, :]).astype(np.float32)
    width = (n_heads // 2) * LANES
    sq = np.zeros((LANES, width), np.float32)
    sk = np.zeros_like(sq)
    constq = np.zeros((1, width), np.float32)
    constk = np.zeros_like(constq)
    for h in range(n_heads):
        base = (h // 2) * LANES + (head_dim if h % 2 == 0 else 0)
        for part in range(DECAY_PARTS):
            sq[part * n_heads + h, base + part] = 1.0
            sk[part * n_heads + h, base + DECAY_PARTS + part] = -1.0
            constq[0, base + DECAY_PARTS + part] = 1.0
            constk[0, base + part] = 1.0
    return (jnp.asarray(tri, _BF16), jnp.asarray(sq, _BF16), jnp.asarray(sk, _BF16),
            jnp.asarray(constq), jnp.asarray(constk))


def _fox_kernel(q_ref, eq_ref, k_ref, ek_ref, v_ref, o_ref, *, seq, tq, tk, head_dim):
    lane = lax.broadcasted_iota(jnp.int32, (1, 2 * head_dim), 1)
    first = lane < head_dim
    pair_w = 2 * head_dim
    ones_col = jnp.where(lax.broadcasted_iota(jnp.int32, (tk, LANES), 1) == 0, 1.0, 0.0).astype(_BF16)

    def update(m, acc, q_rows, k_rows, v_rows, row_offset):
        s = _dot_nt(q_rows, k_rows)
        if row_offset is not None:
            rows = lax.broadcasted_iota(jnp.int32, s.shape, 0)
            cols = lax.broadcasted_iota(jnp.int32, s.shape, 1)
            s = jnp.where(rows + row_offset >= cols, s, MASK_VALUE)
        m_new = jnp.maximum(m, jnp.max(s, axis=1, keepdims=True))
        alpha = jnp.exp2(m - m_new)
        p = jnp.exp2(s - m_new).astype(_BF16)
        return m_new, alpha * acc + _dot(p, v_rows)

    n_q = seq // tq
    q_aug, state = [], []
    for qi in range(n_q):
        qs = slice(qi * tq, (qi + 1) * tq)
        q = q_ref[qs, :]
        eq = eq_ref[qs, :]
        q_aug.append((jnp.where(first, q, eq), jnp.where(first, eq, q)))
        state.append([(jnp.full((tq, 1), MASK_VALUE, _F32),
                       jnp.zeros((tq, pair_w + LANES), _F32)) for _ in range(2)])

    for j in range(n_q):
        ks = slice(j * tk, (j + 1) * tk)
        kc = k_ref[ks, :]
        v_aug = jnp.concatenate([v_ref[ks, :], ones_col], axis=1)
        ekc = ek_ref[ks, :]
        k_aug = (jnp.where(first, kc, ekc), jnp.where(first, ekc, kc))
        for qi in range(j, n_q):
            for e in range(2):
                m, acc = state[qi][e]
                state[qi][e] = update(m, acc, q_aug[qi][e], k_aug[e], v_aug,
                                      0 if j == qi else None)

    for qi in range(n_q):
        acc0, acc1 = state[qi][0][1], state[qi][1][1]
        o_ref[qi * tq:(qi + 1) * tq, :] = jnp.where(
            first, acc0[:, :pair_w] / acc0[:, pair_w:pair_w + 1],
            acc1[:, :pair_w] / acc1[:, pair_w:pair_w + 1]).astype(_BF16)


def _fox(qkv, eq, ek, *, batch, seq, n_heads, head_dim):
    n_tok = qkv.shape[0]
    fox_w = n_heads * head_dim
    n_pairs = n_heads // 2
    pair_w = 2 * head_dim
    tq, tk = ATTN_Q_TILE, ATTN_K_TILE
    assert tq == tk and pair_w == LANES
    return pl.pallas_call(
        functools.partial(_fox_kernel, seq=seq, tq=tq, tk=tk, head_dim=head_dim),
        grid=(batch, n_pairs),
        in_specs=[
            pl.BlockSpec((seq, pair_w), lambda b, p: (b, p)),
            pl.BlockSpec((seq, LANES), lambda b, p: (b, p)),
            pl.BlockSpec((seq, pair_w), lambda b, p: (b, n_pairs + p)),
            pl.BlockSpec((seq, LANES), lambda b, p: (b, p)),
            pl.BlockSpec((seq, pair_w), lambda b, p: (b, 2 * n_pairs + p)),
        ],
        out_specs=pl.BlockSpec((seq, pair_w), lambda b, p: (b, p)),
        out_shape=jax.ShapeDtypeStruct((n_tok, fox_w), _BF16),
        compiler_params=_params("arbitrary", "arbitrary"),
        name="fox",
    )(qkv, eq, qkv, ek, qkv)


def _post_kernel(a_ref, cc_ref, gate_ref, h_ref, wua_ref, wub_ref, wo_ref, gq_ref, wcq_ref,
                 cqg_ref, km_ref, vm_ref, wco_ref, o_ref, merged_s, *, n_heads, head_dim):
    d_model = h_ref.shape[-1]
    scale = head_dim ** -0.5
    a = a_ref[...]
    cc = cc_ref[...]
    for c in range(d_model // MERGE_CHUNK):
        cs = slice(c * MERGE_CHUNK, (c + 1) * MERGE_CHUNK)
        up_a = _dot(a, wua_ref[:, cs])
        up_b = _dot(cc, wub_ref[:, cs])
        merged_s[:, cs] = (gate_ref[:, cs].astype(_F32) * up_a
                           + gate_ref[:, d_model + cs.start:d_model + cs.stop].astype(_F32) * up_b
                           ).astype(_BF16)
    h1 = h_ref[...] + _dot(merged_s[...], wo_ref[...])

    u = _rms_rows(h1, gq_ref[...]).astype(_BF16)
    qm = _dot(u, wcq_ref[...])
    heads = []
    for e in range(n_heads):
        sl = slice(e * head_dim, (e + 1) * head_dim)
        qe = qm[:, sl]
        ms = jnp.mean(qe * qe, axis=-1, keepdims=True)
        qn = (qe * lax.rsqrt(ms + EPS) * cqg_ref[...]).astype(_BF16)
        s = _dot_nt(qn, km_ref[:, sl]) * scale
        p = jnp.exp(s - jnp.max(s, axis=1, keepdims=True))
        den = jnp.sum(p, axis=1, keepdims=True)
        heads.append((_dot(p.astype(_BF16), vm_ref[:, sl]) / den).astype(_BF16))
    o_ref[...] = h1 + _dot(jnp.concatenate(heads, axis=1), wco_ref[...])


def _post(a_out, cc, gates, h, layer, w_up_a, w_up_b, w_o, gq, w_cq, cq_gain_l, kmem, vmem, w_co,
          *, batch, seq):
    n_tok, d_model = h.shape
    tm = TOKEN_TILE
    nt = seq // tm
    fox_w = a_out.shape[1]
    conv_wd = cc.shape[1]
    mem_w = kmem.shape[-1]
    m_tok = kmem.shape[1] // batch
    head_dim = cq_gain_l.shape[-1]
    row = lambda b, j: (b * nt + j, 0)
    const2 = lambda b, j: (0, 0)
    lay3 = lambda b, j: (layer, 0, 0)
    return pl.pallas_call(
        functools.partial(_post_kernel, n_heads=mem_w // head_dim, head_dim=head_dim),
        grid=(batch, nt),
        in_specs=[
            pl.BlockSpec((tm, fox_w), row),
            pl.BlockSpec((tm, conv_wd), row),
            pl.BlockSpec((tm, 2 * d_model), row),
            pl.BlockSpec((tm, d_model), row),
            _resident((None, fox_w, d_model), lay3),
            _resident((None, conv_wd, d_model), lay3),
            _resident((None, d_model, d_model), lay3),
            _resident((1, d_model), const2),
            _resident((None, d_model, mem_w), lay3),
            _resident((1, head_dim), const2),
            pl.BlockSpec((None, m_tok, mem_w), lambda b, j: (layer, b, 0)),
            pl.BlockSpec((None, m_tok, mem_w), lambda b, j: (layer, b, 0)),
            _resident((None, mem_w, d_model), lay3),
        ],
        out_specs=pl.BlockSpec((tm, d_model), row),
        out_shape=jax.ShapeDtypeStruct((n_tok, d_model), _F32),
        scratch_shapes=[pltpu.VMEM((tm, d_model), _BF16)],
        compiler_params=_params("arbitrary", "arbitrary"),
        name="post",
    )(a_out, cc, gates, h, w_up_a, w_up_b, w_o, gq, w_cq, cq_gain_l, kmem, vmem, w_co)


def _ffn_kernel(h_ref, g_ref, wgu_ref, wd_ref, o_ref, act_s, *, d_ff, chunk):
    h = h_ref[...]
    u = _rms_rows(h, g_ref[...]).astype(_BF16)
    for c in range(d_ff // chunk):
        gate = _dot(u, wgu_ref[:, c * chunk:(c + 1) * chunk])
        up = _dot(u, wgu_ref[:, d_ff + c * chunk:d_ff + (c + 1) * chunk])
        act_s[:, c * chunk:(c + 1) * chunk] = (gate * jax.nn.sigmoid(gate) * up).astype(_BF16)
    o_ref[...] = h + _dot(act_s[...], wd_ref[...])


def _ffn(h, layer, gain, w_gu, w_down):
    n_tok, d_model = h.shape
    d_ff = w_down.shape[1]
    tm = TOKEN_TILE
    assert d_ff % FFN_CHUNK == 0
    return pl.pallas_call(
        functools.partial(_ffn_kernel, d_ff=d_ff, chunk=FFN_CHUNK),
        grid=(n_tok // tm,),
        in_specs=[
            pl.BlockSpec((tm, d_model), lambda i: (i, 0)),
            _resident((1, d_model), lambda i: (0, 0)),
            _resident((None, d_model, 2 * d_ff), lambda i: (layer, 0, 0)),
            _resident((None, d_ff, d_model), lambda i: (layer, 0, 0)),
        ],
        out_specs=pl.BlockSpec((tm, d_model), lambda i: (i, 0)),
        out_shape=jax.ShapeDtypeStruct((n_tok, d_model), _F32),
        scratch_shapes=[pltpu.VMEM((tm, d_ff), _BF16)],
        compiler_params=_params("arbitrary"),
        name="ffn",
    )(h, gain, w_gu, w_down)


def kernel(x, mem, norm_mix, w_in, b_f, q_gain, k_gain, conv_w, w_up_a, w_up_b, w_o,
           norm_mem_q, norm_mem_kv, w_cq, w_ckv, cq_gain, ck_gain, w_co, norm_ffn, w_gu, w_down):
    batch, seq, d_model = x.shape
    n_layers = w_in.shape[0]
    head_dim = q_gain.shape[-1]
    n_heads = b_f.shape[-1]
    fox_w = n_heads * head_dim
    conv_wd = conv_w.shape[-1]
    gate_w = 2 * d_model
    assert seq % TOKEN_TILE == 0 and seq % ATTN_Q_TILE == 0 and seq % CUMSUM_BLOCK == 0
    assert w_in.shape[-1] == 3 * fox_w + n_heads + 3 * conv_wd + gate_w

    f0 = 3 * fox_w
    w_in_bf = lax.optimization_barrier(w_in.astype(_BF16))
    w_qkv = w_in_bf[..., :f0]
    w_rest = w_in_bf[..., f0 + n_heads:]
    f_pad = LANES - DECAY_PARTS * n_heads
    w_f = jnp.pad(jnp.tile(w_in[..., f0:f0 + n_heads], (1, 1, DECAY_PARTS)),
                  ((0, 0), (0, 0), (0, f_pad))).astype(_BF16)
    bf_pad = jnp.pad(jnp.tile(b_f, (1, DECAY_PARTS)), ((0, 0), (0, f_pad)))
    w_up_a_bf, w_up_b_bf, w_o_bf = (w.astype(_BF16) for w in (w_up_a, w_up_b, w_o))
    w_cq_bf, w_ckv_bf, w_co_bf = (w.astype(_BF16) for w in (w_cq, w_ckv, w_co))
    w_gu_bf, w_down_bf = w_gu.astype(_BF16), w_down.astype(_BF16)
    qg = jnp.tile(q_gain, (1, n_heads)) * (head_dim ** -0.5 * LOG2E)
    kg = jnp.tile(k_gain, (1, n_heads))

    decay_consts = _decay_constants(n_heads, head_dim)

    kmem, vmem = _memkv(mem, norm_mem_kv, w_ckv_bf, ck_gain)

    h = x.reshape(batch * seq, d_model)
    dims = (fox_w, conv_wd, gate_w, head_dim)
    for l in range(n_layers):
        qkv, logf, cc, gates = _inproj(h, l, norm_mix[l][None], w_qkv, w_rest, w_f, bf_pad[l][None],
                                       qg[l][None], kg[l][None], conv_w[l],
                                       batch=batch, seq=seq, dims=dims)
        eq, ek = _decay(logf, decay_consts, batch=batch, seq=seq, n_heads=n_heads)
        a_out = _fox(qkv, eq, ek, batch=batch, seq=seq, n_heads=n_heads, head_dim=head_dim)
        h = _post(a_out, cc, gates, h, l, w_up_a_bf, w_up_b_bf, w_o_bf, norm_mem_q[l][None],
                  w_cq_bf, cq_gain[l][None], kmem, vmem, w_co_bf, batch=batch, seq=seq)
        h = _ffn(h, l, norm_ffn[l][None], w_gu_bf, w_down_bf)
    return h.reshape(batch, seq, d_model)
```

```python
import functools

import jax
import jax.numpy as jnp
import numpy as np
from jax import lax
from jax.experimental import pallas as pl
from jax.experimental.pallas import tpu as pltpu

_BF16 = jnp.bfloat16
_F32 = jnp.float32

EPS = 1e-6
LOG2E = 1.4426950408889634
LANES = 128
SUBLANES = 8
VMEM_LIMIT_BYTES = 56 * 1024 * 1024
MASK_VALUE = -1e30
TOKEN_TILE = 1024
MERGE_CHUNK = 256
ATTN_Q_TILE = 512
ATTN_K_TILE = 512
CUMSUM_BLOCK = 256
DECAY_PARTS = 3
FFN_CHUNK = 256


def _dot(a, b):
    return jnp.dot(a, b, preferred_element_type=_F32)


def _dot_nt(a, b):
    return lax.dot_general(a, b, (((1,), (1,)), ((), ())), preferred_element_type=_F32)


def _rms_rows(x, gain):
    ms = jnp.mean(x * x, axis=-1, keepdims=True)
    return x * lax.rsqrt(ms + EPS) * gain


def _split3(x):
    hi = x.astype(_BF16)
    r1 = x - hi.astype(_F32)
    mid = r1.astype(_BF16)
    lo = (r1 - mid.astype(_F32)).astype(_BF16)
    return hi, mid, lo


def _params(*sem):
    return pltpu.CompilerParams(dimension_semantics=sem, vmem_limit_bytes=VMEM_LIMIT_BYTES)


def _resident(block_shape, index_map):
    return pl.BlockSpec(block_shape, index_map, pipeline_mode=pl.Buffered(1))


def _memkv_kernel(mem_ref, g_ref, w_ref, ckg_ref, k_ref, v_ref, *, n_heads, head_dim):
    u = _rms_rows(mem_ref[...], g_ref[...]).astype(_BF16)
    kv = _dot(u, w_ref[...])
    width = n_heads * head_dim
    for e in range(n_heads):
        ke = kv[:, e * head_dim:(e + 1) * head_dim]
        ms = jnp.mean(ke * ke, axis=-1, keepdims=True)
        k_ref[:, e * head_dim:(e + 1) * head_dim] = (
            ke * lax.rsqrt(ms + EPS) * ckg_ref[...]).astype(_BF16)
    v_ref[...] = kv[:, width:].astype(_BF16)


def _memkv(mem, norm_mem_kv, w_ckv_bf, ck_gain):
    n_layers, d_model, two_w = w_ckv_bf.shape
    batch, m_tok, _ = mem.shape
    rows = batch * m_tok
    width = two_w // 2
    head_dim = ck_gain.shape[-1]
    n_heads = width // head_dim
    out = jax.ShapeDtypeStruct((n_layers, rows, width), _BF16)
    return pl.pallas_call(
        functools.partial(_memkv_kernel, n_heads=n_heads, head_dim=head_dim),
        grid=(n_layers,),
        in_specs=[
            _resident((rows, d_model), lambda l: (0, 0)),
            pl.BlockSpec((None, 1, d_model), lambda l: (l, 0, 0)),
            pl.BlockSpec((None, d_model, two_w), lambda l: (l, 0, 0)),
            pl.BlockSpec((None, 1, head_dim), lambda l: (l, 0, 0)),
        ],
        out_specs=[
            pl.BlockSpec((None, rows, width), lambda l: (l, 0, 0)),
            pl.BlockSpec((None, rows, width), lambda l: (l, 0, 0)),
        ],
        out_shape=[out, out],
        compiler_params=_params("arbitrary"),
        name="memkv",
    )(mem.reshape(rows, d_model), norm_mem_kv[:, None, :], w_ckv_bf, ck_gain[:, None, :])


def _inproj_kernel(h_ref, g_ref, wqkv_ref, wrest_ref, wf_ref, bf_ref, qg_ref, kg_ref, cw_ref,
                   qkv_ref, lf_ref, cc_ref, gate_ref, tail_s,
                   *, tm, fox_w, conv_w, gate_w, head_dim):
    j = pl.program_id(1)
    u = _rms_rows(h_ref[...], g_ref[...]).astype(_BF16)

    assert 2 * head_dim == LANES
    first = lax.broadcasted_iota(jnp.int32, (1, LANES), 1) < head_dim
    for idx, gain_ref in ((0, qg_ref), (1, kg_ref)):
        a = _dot_nt(u, wqkv_ref[idx * fox_w:(idx + 1) * fox_w, :])
        for c in range(fox_w // LANES):
            cs = slice(c * LANES, (c + 1) * LANES)
            ac = a[:, cs]
            sq = ac * ac
            lo = jnp.sum(jnp.where(first, sq, 0.0), axis=1, keepdims=True)
            hi = jnp.sum(jnp.where(first, 0.0, sq), axis=1, keepdims=True)
            ss = jnp.where(first, lo, hi)
            qkv_ref[:, idx * fox_w + cs.start:idx * fox_w + cs.stop] = (
                ac * lax.rsqrt(ss * (1.0 / head_dim) + EPS) * gain_ref[:, cs]).astype(_BF16)
    qkv_ref[:, 2 * fox_w:3 * fox_w] = _dot_nt(u, wqkv_ref[2 * fox_w:3 * fox_w, :]).astype(_BF16)

    @pl.when(j == 0)
    def _():
        tail_s[...] = jnp.zeros(tail_s.shape, _F32)

    n_taps = cw_ref.shape[0]
    hr = tail_s.shape[0]
    assert n_taps - 1 <= hr and conv_w % (2 * LANES) == 0
    sub = lax.broadcasted_iota(jnp.int32, (hr, LANES), 0)
    for cp in range(conv_w // (2 * LANES)):
        gate_b = _dot_nt(u, wrest_ref[conv_w + cp * 2 * LANES:conv_w + (cp + 1) * 2 * LANES, :])
        for half in range(2):
            cs = slice((2 * cp + half) * LANES, (2 * cp + half + 1) * LANES)
            w_pair = jnp.concatenate(
                [wrest_ref[cs, :], wrest_ref[2 * conv_w + cs.start:2 * conv_w + cs.stop, :]], axis=0)
            r = _dot_nt(u, w_pair)
            zc = r[:, :LANES] * r[:, LANES:]
            y = cw_ref[n_taps - 1:n_taps, cs] * zc
            head = zc[0:hr, :]
            y_head = cw_ref[n_taps - 1:n_taps, cs] * head
            prev = tail_s[:, cs]
            for d in range(1, n_taps):
                w_d = cw_ref[n_taps - 1 - d:n_taps - d, cs]
                y = y + w_d * pltpu.roll(zc, d, axis=0)
                y_head = y_head + w_d * jnp.where(sub < d, pltpu.roll(prev, d, axis=0),
                                                  pltpu.roll(head, d, axis=0))
            gb = gate_b[:, half * LANES:(half + 1) * LANES]
            cc_ref[0:hr, cs] = (gb[0:hr, :] * y_head).astype(_BF16)
            cc_ref[hr:tm, cs] = (gb[hr:tm, :] * y[hr:tm, :]).astype(_BF16)
            tail_s[:, cs] = zc[tm - hr:tm, :]

    o = 3 * conv_w
    step = conv_w
    for i in range(gate_w // step):
        a = _dot_nt(u, wrest_ref[o + i * step:o + (i + 1) * step, :])
        gate_ref[:, i * step:(i + 1) * step] = jax.nn.sigmoid(a).astype(_BF16)

    a = _dot_nt(u, wf_ref[...]) + bf_ref[...]
    lf_ref[...] = jnp.minimum(a, 0.0) - jnp.log1p(jnp.exp(-jnp.abs(a)))


def _inproj(h, layer, gain, w_qkv, w_rest, w_f, bf_pad, qg, kg, conv_w_l, *, batch, seq, dims):
    fox_w, conv_wd, gate_w, head_dim = dims
    n_tok, d_model = h.shape
    tm = TOKEN_TILE
    nt = seq // tm
    row = lambda b, j: (b * nt + j, 0)
    const2 = lambda b, j: (0, 0)
    lay3 = lambda b, j: (layer, 0, 0)
    return pl.pallas_call(
        functools.partial(_inproj_kernel, tm=tm, fox_w=fox_w, conv_w=conv_wd, gate_w=gate_w,
                          head_dim=head_dim),
        grid=(batch, nt),
        in_specs=[
            pl.BlockSpec((tm, d_model), row),
            _resident((1, d_model), const2),
            _resident((None, w_qkv.shape[1], d_model), lay3),
            _resident((None, w_rest.shape[1], d_model), lay3),
            _resident((None, LANES, d_model), lay3),
            _resident((1, LANES), const2),
            _resident((1, fox_w), const2),
            _resident((1, fox_w), const2),
            _resident(conv_w_l.shape, const2),
        ],
        out_specs=[
            pl.BlockSpec((tm, 3 * fox_w), row),
            pl.BlockSpec((tm, LANES), row),
            pl.BlockSpec((tm, conv_wd), row),
            pl.BlockSpec((tm, gate_w), row),
        ],
        out_shape=[
            jax.ShapeDtypeStruct((n_tok, 3 * fox_w), _BF16),
            jax.ShapeDtypeStruct((n_tok, LANES), _F32),
            jax.ShapeDtypeStruct((n_tok, conv_wd), _BF16),
            jax.ShapeDtypeStruct((n_tok, gate_w), _BF16),
        ],
        scratch_shapes=[pltpu.VMEM((2 * SUBLANES, conv_wd), _F32)],
        compiler_params=_params("arbitrary", "arbitrary"),
        name="inproj",
    )(h, gain, w_qkv, w_rest, w_f, bf_pad, qg, kg, conv_w_l)


def _decay_kernel(lf_ref, tri_ref, sq_ref, sk_ref, constq_ref, constk_ref, eq_ref, ek_ref,
                  *, seq, blk, n_heads):
    tri = tri_ref[...]
    lane = lax.broadcasted_iota(jnp.int32, (1, LANES), 1)
    carry = jnp.zeros((1, LANES), _F32)
    for i in range(seq // blk):
        rows = slice(i * blk, (i + 1) * blk)
        hi, mid, lo = _split3(lf_ref[rows, :])
        c = (_dot(tri, lo) + _dot(tri, mid)) + _dot(tri, hi) + carry
        carry = c[blk - 1:blk, :]
        hi, mid, lo = _split3(c * LOG2E)
        parts = jnp.where(lane < n_heads, hi, jnp.where(lane < 2 * n_heads, mid, lo))
        eq_ref[rows, :] = (_dot(parts, sq_ref[...]) + constq_ref[...]).astype(_BF16)
        ek_ref[rows, :] = (_dot(parts, sk_ref[...]) + constk_ref[...]).astype(_BF16)


def _decay(logf, consts, *, batch, seq, n_heads):
    tri, sq = consts[0], consts[1]
    blk = tri.shape[0]
    aug_w = sq.shape[1]
    n_tok = logf.shape[0]
    const = lambda b: (0, 0)
    out = jax.ShapeDtypeStruct((n_tok, aug_w), _BF16)
    return pl.pallas_call(
        functools.partial(_decay_kernel, seq=seq, blk=blk, n_heads=n_heads),
        grid=(batch,),
        in_specs=[pl.BlockSpec((seq, LANES), lambda b: (b, 0))]
        + [_resident(c.shape, const) for c in consts],
        out_specs=[pl.BlockSpec((seq, aug_w), lambda b: (b, 0))] * 2,
        out_shape=[out, out],
        compiler_params=_params("arbitrary"),
        name="decay",
    )(logf, *consts)


def _decay_constants(n_heads, head_dim):
    assert DECAY_PARTS * n_heads <= LANES and 2 * DECAY_PARTS <= head_dim
    r = np.arange(CUMSUM_BLOCK)
    tri = (r[:, None] >= r[None, :]).astype(np.float32)
    width = (n_heads // 2) * LANES
    sq = np.zeros((LANES, width), np.float32)
    sk = np.zeros_like(sq)
    constq = np.zeros((1, width), np.float32)
    constk = np.zeros_like(constq)
    for h in range(n_heads):
        base = (h // 2) * LANES + (head_dim if h % 2 == 0 else 0)
        for part in range(DECAY_PARTS):
            sq[part * n_heads + h, base + part] = 1.0
            sk[part * n_heads + h, base + DECAY_PARTS + part] = -1.0
            constq[0, base + DECAY_PARTS + part] = 1.0
            constk[0, base + part] = 1.0
    return (jnp.asarray(tri, _BF16), jnp.asarray(sq, _BF16), jnp.asarray(sk, _BF16),
            jnp.asarray(constq), jnp.asarray(constk))


def _fox_kernel(q_ref, eq_ref, k_ref, ek_ref, v_ref, o_ref, *, seq, tq, tk, head_dim):
    lane = lax.broadcasted_iota(jnp.int32, (1, 2 * head_dim), 1)
    first = lane < head_dim
    pair_w = 2 * head_dim
    ones_col = jnp.where(lax.broadcasted_iota(jnp.int32, (tk, LANES), 1) == 0, 1.0, 0.0).astype(_BF16)

    def update(m, acc, q_rows, k_rows, v_rows, row_offset):
        s = _dot_nt(q_rows, k_rows)
        if row_offset is not None:
            rows = lax.broadcasted_iota(jnp.int32, s.shape, 0)
            cols = lax.broadcasted_iota(jnp.int32, s.shape, 1)
            s = jnp.where(rows + row_offset >= cols, s, MASK_VALUE)
        m_new = jnp.maximum(m, jnp.max(s, axis=1, keepdims=True))
        alpha = jnp.exp2(m - m_new)
        p = jnp.exp2(s - m_new).astype(_BF16)
        return m_new, alpha * acc + _dot(p, v_rows)

    n_q = seq // tq
    q_aug, state = [], []
    for qi in range(n_q):
        qs = slice(qi * tq, (qi + 1) * tq)
        q = q_ref[qs, :]
        eq = eq_ref[qs, :]
        q_aug.append((jnp.where(first, q, eq), jnp.where(first, eq, q)))
        state.append([(jnp.full((tq, 1), MASK_VALUE, _F32),
                       jnp.zeros((tq, pair_w + LANES), _F32)) for _ in range(2)])

    for j in range(n_q):
        ks = slice(j * tk, (j + 1) * tk)
        kc = k_ref[ks, :]
        v_aug = jnp.concatenate([v_ref[ks, :], ones_col], axis=1)
        ekc = ek_ref[ks, :]
        k_aug = (jnp.where(first, kc, ekc), jnp.where(first, ekc, kc))
        for qi in range(j, n_q):
            for e in range(2):
                m, acc = state[qi][e]
                state[qi][e] = update(m, acc, q_aug[qi][e], k_aug[e], v_aug,
                                      0 if j == qi else None)

    for qi in range(n_q):
        acc0, acc1 = state[qi][0][1], state[qi][1][1]
        o_ref[qi * tq:(qi + 1) * tq, :] = jnp.where(
            first, acc0[:, :pair_w] / acc0[:, pair_w:pair_w + 1],
            acc1[:, :pair_w] / acc1[:, pair_w:pair_w + 1]).astype(_BF16)


def _fox(qkv, eq, ek, *, batch, seq, n_heads, head_dim):
    n_tok = qkv.shape[0]
    fox_w = n_heads * head_dim
    n_pairs = n_heads // 2
    pair_w = 2 * head_dim
    tq, tk = ATTN_Q_TILE, ATTN_K_TILE
    assert tq == tk and pair_w == LANES
    return pl.pallas_call(
        functools.partial(_fox_kernel, seq=seq, tq=tq, tk=tk, head_dim=head_dim),
        grid=(batch, n_pairs),
        in_specs=[
            pl.BlockSpec((seq, pair_w), lambda b, p: (b, p)),
            pl.BlockSpec((seq, LANES), lambda b, p: (b, p)),
            pl.BlockSpec((seq, pair_w), lambda b, p: (b, n_pairs + p)),
            pl.BlockSpec((seq, LANES), lambda b, p: (b, p)),
            pl.BlockSpec((seq, pair_w), lambda b, p: (b, 2 * n_pairs + p)),
        ],
        out_specs=pl.BlockSpec((seq, pair_w), lambda b, p: (b, p)),
        out_shape=jax.ShapeDtypeStruct((n_tok, fox_w), _BF16),
        compiler_params=_params("arbitrary", "arbitrary"),
        name="fox",
    )(qkv, eq, qkv, ek, qkv)


def _post_kernel(a_ref, cc_ref, gate_ref, h_ref, wua_ref, wub_ref, wo_ref, gq_ref, wcq_ref,
                 cqg_ref, km_ref, vm_ref, wco_ref, o_ref, merged_s, *, n_heads, head_dim):
    d_model = h_ref.shape[-1]
    scale = head_dim ** -0.5
    a = a_ref[...]
    cc = cc_ref[...]
    for c in range(d_model // MERGE_CHUNK):
        cs = slice(c * MERGE_CHUNK, (c + 1) * MERGE_CHUNK)
        up_a = _dot(a, wua_ref[:, cs])
        up_b = _dot(cc, wub_ref[:, cs])
        merged_s[:, cs] = (gate_ref[:, cs].astype(_F32) * up_a
                           + gate_ref[:, d_model + cs.start:d_model + cs.stop].astype(_F32) * up_b
                           ).astype(_BF16)
    h1 = h_ref[...] + _dot(merged_s[...], wo_ref[...])

    u = _rms_rows(h1, gq_ref[...]).astype(_BF16)
    qm = _dot(u, wcq_ref[...])
    heads = []
    for e in range(n_heads):
        sl = slice(e * head_dim, (e + 1) * head_dim)
        qe = qm[:, sl]
        ms = jnp.mean(qe * qe, axis=-1, keepdims=True)
        qn = (qe * lax.rsqrt(ms + EPS) * cqg_ref[...]).astype(_BF16)
        s = _dot_nt(qn, km_ref[:, sl]) * scale
        p = jnp.exp(s - jnp.max(s, axis=1, keepdims=True))
        den = jnp.sum(p, axis=1, keepdims=True)
        heads.append((_dot(p.astype(_BF16), vm_ref[:, sl]) / den).astype(_BF16))
    o_ref[...] = h1 + _dot(jnp.concatenate(heads, axis=1), wco_ref[...])


def _post(a_out, cc, gates, h, layer, w_up_a, w_up_b, w_o, gq, w_cq, cq_gain_l, kmem, vmem, w_co,
          *, batch, seq):
    n_tok, d_model = h.shape
    tm = TOKEN_TILE
    nt = seq // tm
    fox_w = a_out.shape[1]
    conv_wd = cc.shape[1]
    mem_w = kmem.shape[-1]
    m_tok = kmem.shape[1] // batch
    head_dim = cq_gain_l.shape[-1]
    row = lambda b, j: (b * nt + j, 0)
    const2 = lambda b, j: (0, 0)
    lay3 = lambda b, j: (layer, 0, 0)
    return pl.pallas_call(
        functools.partial(_post_kernel, n_heads=mem_w // head_dim, head_dim=head_dim),
        grid=(batch, nt),
        in_specs=[
            pl.BlockSpec((tm, fox_w), row),
            pl.BlockSpec((tm, conv_wd), row),
            pl.BlockSpec((tm, 2 * d_model), row),
            pl.BlockSpec((tm, d_model), row),
            _resident((None, fox_w, d_model), lay3),
            _resident((None, conv_wd, d_model), lay3),
            _resident((None, d_model, d_model), lay3),
            _resident((1, d_model), const2),
            _resident((None, d_model, mem_w), lay3),
            _resident((1, head_dim), const2),
            pl.BlockSpec((None, m_tok, mem_w), lambda b, j: (layer, b, 0)),
            pl.BlockSpec((None, m_tok, mem_w), lambda b, j: (layer, b, 0)),
            _resident((None, mem_w, d_model), lay3),
        ],
        out_specs=pl.BlockSpec((tm, d_model), row),
        out_shape=jax.ShapeDtypeStruct((n_tok, d_model), _F32),
        scratch_shapes=[pltpu.VMEM((tm, d_model), _BF16)],
        compiler_params=_params("arbitrary", "arbitrary"),
        name="post",
    )(a_out, cc, gates, h, w_up_a, w_up_b, w_o, gq, w_cq, cq_gain_l, kmem, vmem, w_co)


def _ffn_kernel(h_ref, g_ref, wgu_ref, wd_ref, o_ref, act_s, *, d_ff, chunk):
    h = h_ref[...]
    u = _rms_rows(h, g_ref[...]).astype(_BF16)
    for c in range(d_ff // chunk):
        gate = _dot(u, wgu_ref[:, c * chunk:(c + 1) * chunk])
        up = _dot(u, wgu_ref[:, d_ff + c * chunk:d_ff + (c + 1) * chunk])
        act_s[:, c * chunk:(c + 1) * chunk] = (gate * jax.nn.sigmoid(gate) * up).astype(_BF16)
    o_ref[...] = h + _dot(act_s[...], wd_ref[...])


def _ffn(h, layer, gain, w_gu, w_down):
    n_tok, d_model = h.shape
    d_ff = w_down.shape[1]
    tm = TOKEN_TILE
    assert d_ff % FFN_CHUNK == 0
    return pl.pallas_call(
        functools.partial(_ffn_kernel, d_ff=d_ff, chunk=FFN_CHUNK),
        grid=(n_tok // tm,),
        in_specs=[
            pl.BlockSpec((tm, d_model), lambda i: (i, 0)),
            _resident((1, d_model), lambda i: (0, 0)),
            _resident((None, d_model, 2 * d_ff), lambda i: (layer, 0, 0)),
            _resident((None, d_ff, d_model), lambda i: (layer, 0, 0)),
        ],
        out_specs=pl.BlockSpec((tm, d_model), lambda i: (i, 0)),
        out_shape=jax.ShapeDtypeStruct((n_tok, d_model), _F32),
        scratch_shapes=[pltpu.VMEM((tm, d_ff), _BF16)],
        compiler_params=_params("arbitrary"),
        name="ffn",
    )(h, gain, w_gu, w_down)


def kernel(x, mem, norm_mix, w_in, b_f, q_gain, k_gain, conv_w, w_up_a, w_up_b, w_o,
           norm_mem_q, norm_mem_kv, w_cq, w_ckv, cq_gain, ck_gain, w_co, norm_ffn, w_gu, w_down):
    batch, seq, d_model = x.shape
    n_layers = w_in.shape[0]
    head_dim = q_gain.shape[-1]
    n_heads = b_f.shape[-1]
    fox_w = n_heads * head_dim
    conv_wd = conv_w.shape[-1]
    gate_w = 2 * d_model
    assert seq % TOKEN_TILE == 0 and seq % ATTN_Q_TILE == 0 and seq % CUMSUM_BLOCK == 0
    assert w_in.shape[-1] == 3 * fox_w + n_heads + 3 * conv_wd + gate_w

    f0 = 3 * fox_w
    w_in_t = jnp.swapaxes(w_in, 1, 2)
    w_qkv = w_in_t[:, :f0].astype(_BF16)
    w_rest = w_in_t[:, f0 + n_heads:].astype(_BF16)
    f_pad = LANES - DECAY_PARTS * n_heads
    w_f = jnp.pad(jnp.tile(w_in_t[:, f0:f0 + n_heads], (1, DECAY_PARTS, 1)),
                  ((0, 0), (0, f_pad), (0, 0))).astype(_BF16)
    bf_pad = jnp.pad(jnp.tile(b_f, (1, DECAY_PARTS)), ((0, 0), (0, f_pad)))
    w_up_a_bf, w_up_b_bf, w_o_bf = (w.astype(_BF16) for w in (w_up_a, w_up_b, w_o))
    w_cq_bf, w_ckv_bf, w_co_bf = (w.astype(_BF16) for w in (w_cq, w_ckv, w_co))
    w_gu_bf, w_down_bf = w_gu.astype(_BF16), w_down.astype(_BF16)
    qg = jnp.tile(q_gain, (1, n_heads)) * (head_dim ** -0.5 * LOG2E)
    kg = jnp.tile(k_gain, (1, n_heads))

    decay_consts = _decay_constants(n_heads, head_dim)

    kmem, vmem = _memkv(mem, norm_mem_kv, w_ckv_bf, ck_gain)

    h = x.reshape(batch * seq, d_model)
    dims = (fox_w, conv_wd, gate_w, head_dim)
    for l in range(n_layers):
        qkv, logf, cc, gates = _inproj(h, l, norm_mix[l][None], w_qkv, w_rest, w_f, bf_pad[l][None],
                                       qg[l][None], kg[l][None], conv_w[l],
                                       batch=batch, seq=seq, dims=dims)
        eq, ek = _decay(logf, decay_consts, batch=batch, seq=seq, n_heads=n_heads)
        a_out = _fox(qkv, eq, ek, batch=batch, seq=seq, n_heads=n_heads, head_dim=head_dim)
        h = _post(a_out, cc, gates, h, l, w_up_a_bf, w_up_b_bf, w_o_bf, norm_mem_q[l][None],
                  w_cq_bf, cq_gain[l][None], kmem, vmem, w_co_bf, batch=batch, seq=seq)
        h = _ffn(h, l, norm_ffn[l][None], w_gu_bf, w_down_bf)
    return h.reshape(batch, seq, d_model)
```

```python
import functools

import jax
import jax.numpy as jnp
import numpy as np
from jax import lax
from jax.experimental import pallas as pl
from jax.experimental.pallas import tpu as pltpu

_BF16 = jnp.bfloat16
_F32 = jnp.float32

EPS = 1e-6
LOG2E = 1.4426950408889634
LANES = 128
SUBLANES = 8
VMEM_LIMIT_BYTES = 56 * 1024 * 1024
MASK_VALUE = -1e30
TOKEN_TILE = 1024
MERGE_CHUNK = 256
ATTN_Q_TILE = 512
ATTN_K_TILE = 512
CUMSUM_BLOCK = 256
DECAY_PARTS = 3
FFN_CHUNK = 256


def _dot(a, b):
    return jnp.dot(a, b, preferred_element_type=_F32)


def _dot_nt(a, b):
    return lax.dot_general(a, b, (((1,), (1,)), ((), ())), preferred_element_type=_F32)


def _rms_rows(x, gain):
    ms = jnp.mean(x * x, axis=-1, keepdims=True)
    return x * lax.rsqrt(ms + EPS) * gain


def _split3(x):
    hi = x.astype(_BF16)
    r1 = x - hi.astype(_F32)
    mid = r1.astype(_BF16)
    lo = (r1 - mid.astype(_F32)).astype(_BF16)
    return hi, mid, lo


def _params(*sem):
    return pltpu.CompilerParams(dimension_semantics=sem, vmem_limit_bytes=VMEM_LIMIT_BYTES)


def _resident(block_shape, index_map):
    return pl.BlockSpec(block_shape, index_map, pipeline_mode=pl.Buffered(1))


def _cast_staging(sources, n_steps, step_of):
    in_specs, out_specs, out_shapes = [], [], []
    for arr, layer, first_row, n_rows in sources:
        cols = arr.shape[-1]
        slab = n_rows // n_steps
        assert slab * n_steps == n_rows and slab % (2 * SUBLANES) == 0 and first_row % SUBLANES == 0
        in_specs.append(pl.BlockSpec(
            (None, pl.Element(slab), pl.Element(cols)),
            lambda *g, layer=layer, first_row=first_row, slab=slab:
                (layer, pl.multiple_of(first_row + slab * step_of(*g), SUBLANES), 0)))
        out_specs.append(pl.BlockSpec((slab, cols), lambda *g: (step_of(*g), 0)))
        out_shapes.append(jax.ShapeDtypeStruct((n_rows, cols), _BF16))
    return in_specs, out_specs, out_shapes


def _cast_staged(src_refs, dst_refs):
    for src, dst in zip(src_refs, dst_refs, strict=True):
        dst[...] = src[...].astype(_BF16)


def _memkv_kernel(mem_ref, g_ref, w_ref, ckg_ref, k_ref, v_ref, *, n_heads, head_dim):
    u = _rms_rows(mem_ref[...], g_ref[...]).astype(_BF16)
    kv = _dot(u, w_ref[...])
    width = n_heads * head_dim
    for e in range(n_heads):
        ke = kv[:, e * head_dim:(e + 1) * head_dim]
        ms = jnp.mean(ke * ke, axis=-1, keepdims=True)
        k_ref[:, e * head_dim:(e + 1) * head_dim] = (
            ke * lax.rsqrt(ms + EPS) * ckg_ref[...]).astype(_BF16)
    v_ref[...] = kv[:, width:].astype(_BF16)


def _memkv(mem, norm_mem_kv, w_ckv_bf, ck_gain):
    n_layers, d_model, two_w = w_ckv_bf.shape
    batch, m_tok, _ = mem.shape
    rows = batch * m_tok
    width = two_w // 2
    head_dim = ck_gain.shape[-1]
    n_heads = width // head_dim
    out = jax.ShapeDtypeStruct((n_layers, rows, width), _BF16)
    return pl.pallas_call(
        functools.partial(_memkv_kernel, n_heads=n_heads, head_dim=head_dim),
        grid=(n_layers,),
        in_specs=[
            _resident((rows, d_model), lambda l: (0, 0)),
            pl.BlockSpec((None, 1, d_model), lambda l: (l, 0, 0)),
            pl.BlockSpec((None, d_model, two_w), lambda l: (l, 0, 0)),
            pl.BlockSpec((None, 1, head_dim), lambda l: (l, 0, 0)),
        ],
        out_specs=[
            pl.BlockSpec((None, rows, width), lambda l: (l, 0, 0)),
            pl.BlockSpec((None, rows, width), lambda l: (l, 0, 0)),
        ],
        out_shape=[out, out],
        compiler_params=_params("arbitrary"),
        name="memkv",
    )(mem.reshape(rows, d_model), norm_mem_kv[:, None, :], w_ckv_bf, ck_gain[:, None, :])


def _inproj_kernel(h_ref, g_ref, wqkv_ref, wrest_ref, wf_ref, bf_ref, qg_ref, kg_ref, cw_ref,
                   qkv_ref, lf_ref, cc_ref, gate_ref, tail_s,
                   *, tm, fox_w, conv_w, gate_w, head_dim):
    j = pl.program_id(1)
    u = _rms_rows(h_ref[...], g_ref[...]).astype(_BF16)

    assert 2 * head_dim == LANES
    first = lax.broadcasted_iota(jnp.int32, (1, LANES), 1) < head_dim
    for idx, gain_ref in ((0, qg_ref), (1, kg_ref)):
        a = _dot_nt(u, wqkv_ref[idx * fox_w:(idx + 1) * fox_w, :])
        for c in range(fox_w // LANES):
            cs = slice(c * LANES, (c + 1) * LANES)
            ac = a[:, cs]
            sq = ac * ac
            lo = jnp.sum(jnp.where(first, sq, 0.0), axis=1, keepdims=True)
            hi = jnp.sum(jnp.where(first, 0.0, sq), axis=1, keepdims=True)
            ss = jnp.where(first, lo, hi)
            qkv_ref[:, idx * fox_w + cs.start:idx * fox_w + cs.stop] = (
                ac * lax.rsqrt(ss * (1.0 / head_dim) + EPS) * gain_ref[:, cs]).astype(_BF16)
    qkv_ref[:, 2 * fox_w:3 * fox_w] = _dot_nt(u, wqkv_ref[2 * fox_w:3 * fox_w, :]).astype(_BF16)

    @pl.when(j == 0)
    def _():
        tail_s[...] = jnp.zeros(tail_s.shape, _F32)

    n_taps = cw_ref.shape[0]
    hr = tail_s.shape[0]
    assert n_taps - 1 <= hr and conv_w % (2 * LANES) == 0
    sub = lax.broadcasted_iota(jnp.int32, (hr, LANES), 0)
    for cp in range(conv_w // (2 * LANES)):
        gate_b = _dot_nt(u, wrest_ref[conv_w + cp * 2 * LANES:conv_w + (cp + 1) * 2 * LANES, :])
        for half in range(2):
            cs = slice((2 * cp + half) * LANES, (2 * cp + half + 1) * LANES)
            w_pair = jnp.concatenate(
                [wrest_ref[cs, :], wrest_ref[2 * conv_w + cs.start:2 * conv_w + cs.stop, :]], axis=0)
            r = _dot_nt(u, w_pair)
            zc = r[:, :LANES] * r[:, LANES:]
            y = cw_ref[n_taps - 1:n_taps, cs] * zc
            head = zc[0:hr, :]
            y_head = cw_ref[n_taps - 1:n_taps, cs] * head
            prev = tail_s[:, cs]
            for d in range(1, n_taps):
                w_d = cw_ref[n_taps - 1 - d:n_taps - d, cs]
                y = y + w_d * pltpu.roll(zc, d, axis=0)
                y_head = y_head + w_d * jnp.where(sub < d, pltpu.roll(prev, d, axis=0),
                                                  pltpu.roll(head, d, axis=0))
            gb = gate_b[:, half * LANES:(half + 1) * LANES]
            cc_ref[0:hr, cs] = (gb[0:hr, :] * y_head).astype(_BF16)
            cc_ref[hr:tm, cs] = (gb[hr:tm, :] * y[hr:tm, :]).astype(_BF16)
            tail_s[:, cs] = zc[tm - hr:tm, :]

    o = 3 * conv_w
    step = conv_w
    for i in range(gate_w // step):
        a = _dot_nt(u, wrest_ref[o + i * step:o + (i + 1) * step, :])
        gate_ref[:, i * step:(i + 1) * step] = jax.nn.sigmoid(a).astype(_BF16)

    a = _dot_nt(u, wf_ref[...]) + bf_ref[...]
    lf_ref[...] = jnp.minimum(a, 0.0) - jnp.log1p(jnp.exp(-jnp.abs(a)))


def _inproj(h, gain, w_qkv, w_rest, w_f, bf_pad, qg, kg, conv_w_l, *, batch, seq, dims):
    fox_w, conv_wd, gate_w, head_dim = dims
    n_tok, d_model = h.shape
    tm = TOKEN_TILE
    nt = seq // tm
    row = lambda b, j: (b * nt + j, 0)
    const2 = lambda b, j: (0, 0)
    return pl.pallas_call(
        functools.partial(_inproj_kernel, tm=tm, fox_w=fox_w, conv_w=conv_wd, gate_w=gate_w,
                          head_dim=head_dim),
        grid=(batch, nt),
        in_specs=[
            pl.BlockSpec((tm, d_model), row),
            _resident((1, d_model), const2),
            _resident(w_qkv.shape, const2),
            _resident(w_rest.shape, const2),
            _resident(w_f.shape, const2),
            _resident((1, LANES), const2),
            _resident((1, fox_w), const2),
            _resident((1, fox_w), const2),
            _resident(conv_w_l.shape, const2),
        ],
        out_specs=[
            pl.BlockSpec((tm, 3 * fox_w), row),
            pl.BlockSpec((tm, LANES), row),
            pl.BlockSpec((tm, conv_wd), row),
            pl.BlockSpec((tm, gate_w), row),
        ],
        out_shape=[
            jax.ShapeDtypeStruct((n_tok, 3 * fox_w), _BF16),
            jax.ShapeDtypeStruct((n_tok, LANES), _F32),
            jax.ShapeDtypeStruct((n_tok, conv_wd), _BF16),
            jax.ShapeDtypeStruct((n_tok, gate_w), _BF16),
        ],
        scratch_shapes=[pltpu.VMEM((2 * SUBLANES, conv_wd), _F32)],
        compiler_params=_params("arbitrary", "arbitrary"),
        name="inproj",
    )(h, gain, w_qkv, w_rest, w_f, bf_pad, qg, kg, conv_w_l)


def _decay_kernel(lf_ref, tri_ref, sq_ref, sk_ref, constq_ref, constk_ref, eq_ref, ek_ref,
                  *, seq, blk, n_heads):
    tri = tri_ref[...]
    lane = lax.broadcasted_iota(jnp.int32, (1, LANES), 1)
    carry = jnp.zeros((1, LANES), _F32)
    for i in range(seq // blk):
        rows = slice(i * blk, (i + 1) * blk)
        hi, mid, lo = _split3(lf_ref[rows, :])
        c = (_dot(tri, lo) + _dot(tri, mid)) + _dot(tri, hi) + carry
        carry = c[blk - 1:blk, :]
        hi, mid, lo = _split3(c * LOG2E)
        parts = jnp.where(lane < n_heads, hi, jnp.where(lane < 2 * n_heads, mid, lo))
        eq_ref[rows, :] = (_dot(parts, sq_ref[...]) + constq_ref[...]).astype(_BF16)
        ek_ref[rows, :] = (_dot(parts, sk_ref[...]) + constk_ref[...]).astype(_BF16)


def _decay(logf, consts, *, batch, seq, n_heads):
    tri, sq = consts[0], consts[1]
    blk = tri.shape[0]
    aug_w = sq.shape[1]
    n_tok = logf.shape[0]
    const = lambda b: (0, 0)
    out = jax.ShapeDtypeStruct((n_tok, aug_w), _BF16)
    return pl.pallas_call(
        functools.partial(_decay_kernel, seq=seq, blk=blk, n_heads=n_heads),
        grid=(batch,),
        in_specs=[pl.BlockSpec((seq, LANES), lambda b: (b, 0))]
        + [_resident(c.shape, const) for c in consts],
        out_specs=[pl.BlockSpec((seq, aug_w), lambda b: (b, 0))] * 2,
        out_shape=[out, out],
        compiler_params=_params("arbitrary"),
        name="decay",
    )(logf, *consts)


def _decay_constants(n_heads, head_dim):
    assert DECAY_PARTS * n_heads <= LANES and 2 * DECAY_PARTS <= head_dim
    r = np.arange(CUMSUM_BLOCK)
    tri = (r[:, None] >= r[None, :]).astype(np.float32)
    width = (n_heads // 2) * LANES
    sq = np.zeros((LANES, width), np.float32)
    sk = np.zeros_like(sq)
    constq = np.zeros((1, width), np.float32)
    constk = np.zeros_like(constq)
    for h in range(n_heads):
        base = (h // 2) * LANES + (head_dim if h % 2 == 0 else 0)
        for part in range(DECAY_PARTS):
            sq[part * n_heads + h, base + part] = 1.0
            sk[part * n_heads + h, base + DECAY_PARTS + part] = -1.0
            constq[0, base + DECAY_PARTS + part] = 1.0
            constk[0, base + part] = 1.0
    return (jnp.asarray(tri, _BF16), jnp.asarray(sq, _BF16), jnp.asarray(sk, _BF16),
            jnp.asarray(constq), jnp.asarray(constk))


def _fox_kernel(q_ref, eq_ref, k_ref, ek_ref, v_ref, o_ref, *, seq, tq, tk, head_dim):
    lane = lax.broadcasted_iota(jnp.int32, (1, 2 * head_dim), 1)
    first = lane < head_dim
    pair_w = 2 * head_dim
    ones_col = jnp.where(lax.broadcasted_iota(jnp.int32, (tk, LANES), 1) == 0, 1.0, 0.0).astype(_BF16)

    def update(m, acc, q_rows, k_rows, v_rows, row_offset):
        s = _dot_nt(q_rows, k_rows)
        if row_offset is not None:
            rows = lax.broadcasted_iota(jnp.int32, s.shape, 0)
            cols = lax.broadcasted_iota(jnp.int32, s.shape, 1)
            s = jnp.where(rows + row_offset >= cols, s, MASK_VALUE)
        m_new = jnp.maximum(m, jnp.max(s, axis=1, keepdims=True))
        alpha = jnp.exp2(m - m_new)
        p = jnp.exp2(s - m_new).astype(_BF16)
        return m_new, alpha * acc + _dot(p, v_rows)

    n_q = seq // tq
    q_aug, state = [], []
    for qi in range(n_q):
        qs = slice(qi * tq, (qi + 1) * tq)
        q = q_ref[qs, :]
        eq = eq_ref[qs, :]
        q_aug.append((jnp.where(first, q, eq), jnp.where(first, eq, q)))
        state.append([(jnp.full((tq, 1), MASK_VALUE, _F32),
                       jnp.zeros((tq, pair_w + LANES), _F32)) for _ in range(2)])

    for j in range(n_q):
        ks = slice(j * tk, (j + 1) * tk)
        kc = k_ref[ks, :]
        v_aug = jnp.concatenate([v_ref[ks, :], ones_col], axis=1)
        ekc = ek_ref[ks, :]
        k_aug = (jnp.where(first, kc, ekc), jnp.where(first, ekc, kc))
        for qi in range(j, n_q):
            for e in range(2):
                m, acc = state[qi][e]
                state[qi][e] = update(m, acc, q_aug[qi][e], k_aug[e], v_aug,
                                      0 if j == qi else None)

    for qi in range(n_q):
        acc0, acc1 = state[qi][0][1], state[qi][1][1]
        o_ref[qi * tq:(qi + 1) * tq, :] = jnp.where(
            first, acc0[:, :pair_w] / acc0[:, pair_w:pair_w + 1],
            acc1[:, :pair_w] / acc1[:, pair_w:pair_w + 1]).astype(_BF16)


def _fox(qkv, eq, ek, *, batch, seq, n_heads, head_dim):
    n_tok = qkv.shape[0]
    fox_w = n_heads * head_dim
    n_pairs = n_heads // 2
    pair_w = 2 * head_dim
    tq, tk = ATTN_Q_TILE, ATTN_K_TILE
    assert tq == tk and pair_w == LANES
    return pl.pallas_call(
        functools.partial(_fox_kernel, seq=seq, tq=tq, tk=tk, head_dim=head_dim),
        grid=(batch, n_pairs),
        in_specs=[
            pl.BlockSpec((seq, pair_w), lambda b, p: (b, p)),
            pl.BlockSpec((seq, LANES), lambda b, p: (b, p)),
            pl.BlockSpec((seq, pair_w), lambda b, p: (b, n_pairs + p)),
            pl.BlockSpec((seq, LANES), lambda b, p: (b, p)),
            pl.BlockSpec((seq, pair_w), lambda b, p: (b, 2 * n_pairs + p)),
        ],
        out_specs=pl.BlockSpec((seq, pair_w), lambda b, p: (b, p)),
        out_shape=jax.ShapeDtypeStruct((n_tok, fox_w), _BF16),
        compiler_params=_params("arbitrary", "arbitrary"),
        name="fox",
    )(qkv, eq, qkv, ek, qkv)


def _post_kernel(a_ref, cc_ref, gate_ref, h_ref, wua_ref, wub_ref, wo_ref, gq_ref, wcq_ref,
                 cqg_ref, km_ref, vm_ref, wco_ref, *rest, n_heads, head_dim, n_staged):
    staged_src, (o_ref, *staged_dst, merged_s) = rest[:n_staged], rest[n_staged:]
    _cast_staged(staged_src, staged_dst)
    d_model = h_ref.shape[-1]
    scale = head_dim ** -0.5
    a = a_ref[...]
    cc = cc_ref[...]
    for c in range(d_model // MERGE_CHUNK):
        cs = slice(c * MERGE_CHUNK, (c + 1) * MERGE_CHUNK)
        up_a = _dot(a, wua_ref[:, cs])
        up_b = _dot(cc, wub_ref[:, cs])
        merged_s[:, cs] = (gate_ref[:, cs].astype(_F32) * up_a
                           + gate_ref[:, d_model + cs.start:d_model + cs.stop].astype(_F32) * up_b
                           ).astype(_BF16)
    h1 = h_ref[...] + _dot(merged_s[...], wo_ref[...])

    u = _rms_rows(h1, gq_ref[...]).astype(_BF16)
    qm = _dot(u, wcq_ref[...])
    heads = []
    for e in range(n_heads):
        sl = slice(e * head_dim, (e + 1) * head_dim)
        qe = qm[:, sl]
        ms = jnp.mean(qe * qe, axis=-1, keepdims=True)
        qn = (qe * lax.rsqrt(ms + EPS) * cqg_ref[...]).astype(_BF16)
        s = _dot_nt(qn, km_ref[:, sl]) * scale
        p = jnp.exp(s - jnp.max(s, axis=1, keepdims=True))
        den = jnp.sum(p, axis=1, keepdims=True)
        heads.append((_dot(p.astype(_BF16), vm_ref[:, sl]) / den).astype(_BF16))
    o_ref[...] = h1 + _dot(jnp.concatenate(heads, axis=1), wco_ref[...])


def _post(a_out, cc, gates, h, layer, w_up_a, w_up_b, w_o, gq, w_cq, cq_gain_l, kmem, vmem, w_co,
          staged, *, batch, seq):
    n_tok, d_model = h.shape
    tm = TOKEN_TILE
    nt = seq // tm
    fox_w = a_out.shape[1]
    conv_wd = cc.shape[1]
    mem_w = kmem.shape[-1]
    m_tok = kmem.shape[1] // batch
    head_dim = cq_gain_l.shape[-1]
    row = lambda b, j: (b * nt + j, 0)
    const2 = lambda b, j: (0, 0)
    st_in, st_out, st_shapes = _cast_staging(staged, batch * nt, lambda b, j: b * nt + j)
    return pl.pallas_call(
        functools.partial(_post_kernel, n_heads=mem_w // head_dim, head_dim=head_dim,
                          n_staged=len(staged)),
        grid=(batch, nt),
        in_specs=[
            pl.BlockSpec((tm, fox_w), row),
            pl.BlockSpec((tm, conv_wd), row),
            pl.BlockSpec((tm, 2 * d_model), row),
            pl.BlockSpec((tm, d_model), row),
            _resident(w_up_a.shape, const2),
            _resident(w_up_b.shape, const2),
            _resident(w_o.shape, const2),
            _resident((1, d_model), const2),
            _resident(w_cq.shape, const2),
            _resident((1, head_dim), const2),
            pl.BlockSpec((None, m_tok, mem_w), lambda b, j: (layer, b, 0)),
            pl.BlockSpec((None, m_tok, mem_w), lambda b, j: (layer, b, 0)),
            _resident(w_co.shape, const2),
        ] + st_in,
        out_specs=[pl.BlockSpec((tm, d_model), row)] + st_out,
        out_shape=[jax.ShapeDtypeStruct((n_tok, d_model), _F32)] + st_shapes,
        scratch_shapes=[pltpu.VMEM((tm, d_model), _BF16)],
        compiler_params=_params("arbitrary", "arbitrary"),
        name="post",
    )(a_out, cc, gates, h, w_up_a, w_up_b, w_o, gq, w_cq, cq_gain_l, kmem, vmem, w_co,
      *[s[0] for s in staged])


def _ffn_kernel(h_ref, g_ref, wgu_ref, wd_ref, *rest, d_ff, chunk, n_staged):
    staged_src, (o_ref, *staged_dst, act_s) = rest[:n_staged], rest[n_staged:]
    _cast_staged(staged_src, staged_dst)
    h = h_ref[...]
    u = _rms_rows(h, g_ref[...]).astype(_BF16)
    for c in range(d_ff // chunk):
        gate = _dot(u, wgu_ref[:, c * chunk:(c + 1) * chunk])
        up = _dot(u, wgu_ref[:, d_ff + c * chunk:d_ff + (c + 1) * chunk])
        act_s[:, c * chunk:(c + 1) * chunk] = (gate * jax.nn.sigmoid(gate) * up).astype(_BF16)
    o_ref[...] = h + _dot(act_s[...], wd_ref[...])


def _ffn(h, gain, w_gu, w_down, staged):
    n_tok, d_model = h.shape
    d_ff = w_down.shape[0]
    tm = TOKEN_TILE
    assert d_ff % FFN_CHUNK == 0
    st_in, st_out, st_shapes = _cast_staging(staged, n_tok // tm, lambda i: i)
    return pl.pallas_call(
        functools.partial(_ffn_kernel, d_ff=d_ff, chunk=FFN_CHUNK, n_staged=len(staged)),
        grid=(n_tok // tm,),
        in_specs=[
            pl.BlockSpec((tm, d_model), lambda i: (i, 0)),
            _resident((1, d_model), lambda i: (0, 0)),
            _resident(w_gu.shape, lambda i: (0, 0)),
            _resident(w_down.shape, lambda i: (0, 0)),
        ] + st_in,
        out_specs=[pl.BlockSpec((tm, d_model), lambda i: (i, 0))] + st_out,
        out_shape=[jax.ShapeDtypeStruct((n_tok, d_model), _F32)] + st_shapes,
        scratch_shapes=[pltpu.VMEM((tm, d_ff), _BF16)],
        compiler_params=_params("arbitrary"),
        name="ffn",
    )(h, gain, w_gu, w_down, *[s[0] for s in staged])


def kernel(x, mem, norm_mix, w_in, b_f, q_gain, k_gain, conv_w, w_up_a, w_up_b, w_o,
           norm_mem_q, norm_mem_kv, w_cq, w_ckv, cq_gain, ck_gain, w_co, norm_ffn, w_gu, w_down):
    batch, seq, d_model = x.shape
    n_layers = w_in.shape[0]
    head_dim = q_gain.shape[-1]
    n_heads = b_f.shape[-1]
    fox_w = n_heads * head_dim
    conv_wd = conv_w.shape[-1]
    gate_w = 2 * d_model
    assert seq % TOKEN_TILE == 0 and seq % ATTN_Q_TILE == 0 and seq % CUMSUM_BLOCK == 0
    assert w_in.shape[-1] == 3 * fox_w + n_heads + 3 * conv_wd + gate_w

    f0 = 3 * fox_w
    w_in_t = jnp.swapaxes(w_in, 1, 2)
    rest_rows = w_in_t.shape[1] - f0 - n_heads
    f_pad = LANES - DECAY_PARTS * n_heads
    w_f = jnp.pad(jnp.tile(w_in_t[:, f0:f0 + n_heads], (1, DECAY_PARTS, 1)),
                  ((0, 0), (0, f_pad), (0, 0))).astype(_BF16)
    bf_pad = jnp.pad(jnp.tile(b_f, (1, DECAY_PARTS)), ((0, 0), (0, f_pad)))
    w_ckv_bf = w_ckv.astype(_BF16)

    def mixer_sources(layer):
        return [(w_in_t, layer, 0, f0), (w_in_t, layer, f0 + n_heads, rest_rows)] + [
            (w, layer, 0, w.shape[1]) for w in (w_up_a, w_up_b, w_o, w_cq, w_co)]

    mixer_w = [arr[layer, first:first + rows].astype(_BF16)
               for arr, layer, first, rows in mixer_sources(0)]
    qg = jnp.tile(q_gain, (1, n_heads)) * (head_dim ** -0.5 * LOG2E)
    kg = jnp.tile(k_gain, (1, n_heads))

    decay_consts = _decay_constants(n_heads, head_dim)

    kmem, vmem = _memkv(mem, norm_mem_kv, w_ckv_bf, ck_gain)

    h = x.reshape(batch * seq, d_model)
    dims = (fox_w, conv_wd, gate_w, head_dim)
    for l in range(n_layers):
        w_qkv, w_rest, w_up_a_bf, w_up_b_bf, w_o_bf, w_cq_bf, w_co_bf = mixer_w
        qkv, logf, cc, gates = _inproj(h, norm_mix[l][None], w_qkv, w_rest, w_f[l], bf_pad[l][None],
                                       qg[l][None], kg[l][None], conv_w[l],
                                       batch=batch, seq=seq, dims=dims)
        eq, ek = _decay(logf, decay_consts, batch=batch, seq=seq, n_heads=n_heads)
        a_out = _fox(qkv, eq, ek, batch=batch, seq=seq, n_heads=n_heads, head_dim=head_dim)
        ffn_sources = [(w_gu, l, 0, w_gu.shape[1]), (w_down, l, 0, w_down.shape[1])]
        h, w_gu_bf, w_down_bf = _post(
            a_out, cc, gates, h, l, w_up_a_bf, w_up_b_bf, w_o_bf, norm_mem_q[l][None], w_cq_bf,
            cq_gain[l][None], kmem, vmem, w_co_bf, ffn_sources, batch=batch, seq=seq)
        next_sources = mixer_sources(l + 1) if l + 1 < n_layers else []
        h, *mixer_w = _ffn(h, norm_ffn[l][None], w_gu_bf, w_down_bf, next_sources)
    return h.reshape(batch, seq, d_model)
```

```python
import functools

import jax
import jax.numpy as jnp
import numpy as np
from jax import lax
from jax.experimental import pallas as pl
from jax.experimental.pallas import tpu as pltpu

_BF16 = jnp.bfloat16
_F32 = jnp.float32

EPS = 1e-6
LOG2E = 1.4426950408889634
LANES = 128
SUBLANES = 8
VMEM_LIMIT_BYTES = 56 * 1024 * 1024
MASK_VALUE = -1e30
TOKEN_TILE = 1024
MEMKV_SPLIT = 2
MERGE_CHUNK = 256
ATTN_Q_TILE = 512
ATTN_K_TILE = 512
CUMSUM_BLOCK = 256
DECAY_PARTS = 3
FFN_CHUNK = 256


def _dot(a, b):
    return jnp.dot(a, b, preferred_element_type=_F32)


def _dot_nt(a, b):
    return lax.dot_general(a, b, (((1,), (1,)), ((), ())), preferred_element_type=_F32)


def _rms_rows(x, gain):
    ms = jnp.mean(x * x, axis=-1, keepdims=True)
    return x * lax.rsqrt(ms + EPS) * gain


def _split3(x):
    hi = x.astype(_BF16)
    r1 = x - hi.astype(_F32)
    mid = r1.astype(_BF16)
    lo = (r1 - mid.astype(_F32)).astype(_BF16)
    return hi, mid, lo


def _params(*sem):
    return pltpu.CompilerParams(dimension_semantics=sem, vmem_limit_bytes=VMEM_LIMIT_BYTES)


def _resident(block_shape, index_map):
    return pl.BlockSpec(block_shape, index_map, pipeline_mode=pl.Buffered(1))


def _cast_staging(sources, n_steps, step_of):
    in_specs, out_specs, out_shapes = [], [], []
    for arr, layer, first_row, n_rows in sources:
        cols = arr.shape[-1]
        slab = n_rows // n_steps
        assert slab * n_steps == n_rows and slab % (2 * SUBLANES) == 0 and first_row % SUBLANES == 0
        in_specs.append(pl.BlockSpec(
            (None, pl.Element(slab), pl.Element(cols)),
            lambda *g, layer=layer, first_row=first_row, slab=slab:
                (layer, pl.multiple_of(first_row + slab * step_of(*g), SUBLANES), 0)))
        out_specs.append(pl.BlockSpec((slab, cols), lambda *g: (step_of(*g), 0)))
        out_shapes.append(jax.ShapeDtypeStruct((n_rows, cols), _BF16))
    return in_specs, out_specs, out_shapes


def _cast_staged(src_refs, dst_refs):
    for src, dst in zip(src_refs, dst_refs, strict=True):
        dst[...] = src[...].astype(_BF16)


def _memkv_kernel(mem_ref, g_ref, w_ref, ckg_ref, *rest, n_heads, head_dim, n_staged):
    staged_src, (k_ref, v_ref, *staged_dst) = rest[:n_staged], rest[n_staged:]
    _cast_staged(staged_src, staged_dst)
    u = _rms_rows(mem_ref[...], g_ref[...]).astype(_BF16)
    kv = _dot(u, w_ref[...].astype(_BF16))
    width = n_heads * head_dim
    for e in range(n_heads):
        ke = kv[:, e * head_dim:(e + 1) * head_dim]
        ms = jnp.mean(ke * ke, axis=-1, keepdims=True)
        k_ref[:, e * head_dim:(e + 1) * head_dim] = (
            ke * lax.rsqrt(ms + EPS) * ckg_ref[...]).astype(_BF16)
    v_ref[...] = kv[:, width:].astype(_BF16)


def _memkv(mem, norm_mem_kv, w_ckv, ck_gain, staged):
    n_layers, d_model, two_w = w_ckv.shape
    batch, m_tok, _ = mem.shape
    rows = batch * m_tok
    blk = rows // MEMKV_SPLIT
    width = two_w // 2
    head_dim = ck_gain.shape[-1]
    n_heads = width // head_dim
    out = jax.ShapeDtypeStruct((n_layers, rows, width), _BF16)
    st_in, st_out, st_shapes = _cast_staging(staged, n_layers * MEMKV_SPLIT,
                                             lambda l, s: l * MEMKV_SPLIT + s)
    return pl.pallas_call(
        functools.partial(_memkv_kernel, n_heads=n_heads, head_dim=head_dim,
                          n_staged=len(staged)),
        grid=(n_layers, MEMKV_SPLIT),
        in_specs=[
            pl.BlockSpec((blk, d_model), lambda l, s: (s, 0)),
            pl.BlockSpec((None, 1, d_model), lambda l, s: (l, 0, 0)),
            pl.BlockSpec((None, d_model, two_w), lambda l, s: (l, 0, 0)),
            pl.BlockSpec((None, 1, head_dim), lambda l, s: (l, 0, 0)),
        ] + st_in,
        out_specs=[
            pl.BlockSpec((None, blk, width), lambda l, s: (l, s, 0)),
            pl.BlockSpec((None, blk, width), lambda l, s: (l, s, 0)),
        ] + st_out,
        out_shape=[out, out] + st_shapes,
        compiler_params=_params("arbitrary", "arbitrary"),
        name="memkv",
    )(mem.reshape(rows, d_model), norm_mem_kv[:, None, :], w_ckv, ck_gain[:, None, :],
      *[s[0] for s in staged])


def _inproj_kernel(h_ref, g_ref, wqkv_ref, wrest_ref, wf_ref, bf_ref, qg_ref, kg_ref, cw_ref,
                   qkv_ref, lf_ref, cc_ref, gate_ref, tail_s,
                   *, tm, fox_w, conv_w, gate_w, head_dim):
    j = pl.program_id(1)
    u = _rms_rows(h_ref[...], g_ref[...]).astype(_BF16)

    assert 2 * head_dim == LANES
    first = lax.broadcasted_iota(jnp.int32, (1, LANES), 1) < head_dim
    for idx, gain_ref in ((0, qg_ref), (1, kg_ref)):
        a = _dot_nt(u, wqkv_ref[idx * fox_w:(idx + 1) * fox_w, :])
        for c in range(fox_w // LANES):
            cs = slice(c * LANES, (c + 1) * LANES)
            ac = a[:, cs]
            sq = ac * ac
            lo = jnp.sum(jnp.where(first, sq, 0.0), axis=1, keepdims=True)
            hi = jnp.sum(jnp.where(first, 0.0, sq), axis=1, keepdims=True)
            ss = jnp.where(first, lo, hi)
            qkv_ref[:, idx * fox_w + cs.start:idx * fox_w + cs.stop] = (
                ac * lax.rsqrt(ss * (1.0 / head_dim) + EPS) * gain_ref[:, cs]).astype(_BF16)
    qkv_ref[:, 2 * fox_w:3 * fox_w] = _dot_nt(u, wqkv_ref[2 * fox_w:3 * fox_w, :]).astype(_BF16)

    @pl.when(j == 0)
    def _():
        tail_s[...] = jnp.zeros(tail_s.shape, _F32)

    n_taps = cw_ref.shape[0]
    hr = tail_s.shape[0]
    assert n_taps - 1 <= hr and conv_w % (2 * LANES) == 0
    sub = lax.broadcasted_iota(jnp.int32, (hr, LANES), 0)
    for cp in range(conv_w // (2 * LANES)):
        gate_b = _dot_nt(u, wrest_ref[conv_w + cp * 2 * LANES:conv_w + (cp + 1) * 2 * LANES, :])
        for half in range(2):
            cs = slice((2 * cp + half) * LANES, (2 * cp + half + 1) * LANES)
            w_pair = jnp.concatenate(
                [wrest_ref[cs, :], wrest_ref[2 * conv_w + cs.start:2 * conv_w + cs.stop, :]], axis=0)
            r = _dot_nt(u, w_pair)
            zc = r[:, :LANES] * r[:, LANES:]
            y = cw_ref[n_taps - 1:n_taps, cs] * zc
            head = zc[0:hr, :]
            y_head = cw_ref[n_taps - 1:n_taps, cs] * head
            prev = tail_s[:, cs]
            for d in range(1, n_taps):
                w_d = cw_ref[n_taps - 1 - d:n_taps - d, cs]
                y = y + w_d * pltpu.roll(zc, d, axis=0)
                y_head = y_head + w_d * jnp.where(sub < d, pltpu.roll(prev, d, axis=0),
                                                  pltpu.roll(head, d, axis=0))
            gb = gate_b[:, half * LANES:(half + 1) * LANES]
            cc_ref[0:hr, cs] = (gb[0:hr, :] * y_head).astype(_BF16)
            cc_ref[hr:tm, cs] = (gb[hr:tm, :] * y[hr:tm, :]).astype(_BF16)
            tail_s[:, cs] = zc[tm - hr:tm, :]

    o = 3 * conv_w
    step = conv_w
    for i in range(gate_w // step):
        a = _dot_nt(u, wrest_ref[o + i * step:o + (i + 1) * step, :])
        gate_ref[:, i * step:(i + 1) * step] = jax.nn.sigmoid(a).astype(_BF16)

    a = _dot_nt(u, wf_ref[...]) + bf_ref[...]
    lf_ref[...] = jnp.minimum(a, 0.0) - jnp.log1p(jnp.exp(-jnp.abs(a)))


def _inproj(h, gain, w_qkv, w_rest, w_f, bf_pad, qg, kg, conv_w_l, *, batch, seq, dims):
    fox_w, conv_wd, gate_w, head_dim = dims
    n_tok, d_model = h.shape
    tm = TOKEN_TILE
    nt = seq // tm
    row = lambda b, j: (b * nt + j, 0)
    const2 = lambda b, j: (0, 0)
    return pl.pallas_call(
        functools.partial(_inproj_kernel, tm=tm, fox_w=fox_w, conv_w=conv_wd, gate_w=gate_w,
                          head_dim=head_dim),
        grid=(batch, nt),
        in_specs=[
            pl.BlockSpec((tm, d_model), row),
            _resident((1, d_model), const2),
            _resident(w_qkv.shape, const2),
            _resident(w_rest.shape, const2),
            _resident(w_f.shape, const2),
            _resident((1, LANES), const2),
            _resident((1, fox_w), const2),
            _resident((1, fox_w), const2),
            _resident(conv_w_l.shape, const2),
        ],
        out_specs=[
            pl.BlockSpec((tm, 3 * fox_w), row),
            pl.BlockSpec((tm, LANES), row),
            pl.BlockSpec((tm, conv_wd), row),
            pl.BlockSpec((tm, gate_w), row),
        ],
        out_shape=[
            jax.ShapeDtypeStruct((n_tok, 3 * fox_w), _BF16),
            jax.ShapeDtypeStruct((n_tok, LANES), _F32),
            jax.ShapeDtypeStruct((n_tok, conv_wd), _BF16),
            jax.ShapeDtypeStruct((n_tok, gate_w), _BF16),
        ],
        scratch_shapes=[pltpu.VMEM((2 * SUBLANES, conv_wd), _F32)],
        compiler_params=_params("arbitrary", "arbitrary"),
        name="inproj",
    )(h, gain, w_qkv, w_rest, w_f, bf_pad, qg, kg, conv_w_l)


def _decay_kernel(lf_ref, tri_ref, sq_ref, sk_ref, constq_ref, constk_ref, eq_ref, ek_ref,
                  *, seq, blk, n_heads):
    tri = tri_ref[...]
    lane = lax.broadcasted_iota(jnp.int32, (1, LANES), 1)
    carry = jnp.zeros((1, LANES), _F32)
    for i in range(seq // blk):
        rows = slice(i * blk, (i + 1) * blk)
        hi, mid, lo = _split3(lf_ref[rows, :])
        c = (_dot(tri, lo) + _dot(tri, mid)) + _dot(tri, hi) + carry
        carry = c[blk - 1:blk, :]
        hi, mid, lo = _split3(c * LOG2E)
        parts = jnp.where(lane < n_heads, hi, jnp.where(lane < 2 * n_heads, mid, lo))
        eq_ref[rows, :] = (_dot(parts, sq_ref[...]) + constq_ref[...]).astype(_BF16)
        ek_ref[rows, :] = (_dot(parts, sk_ref[...]) + constk_ref[...]).astype(_BF16)


def _decay(logf, consts, *, batch, seq, n_heads):
    tri, sq = consts[0], consts[1]
    blk = tri.shape[0]
    aug_w = sq.shape[1]
    n_tok = logf.shape[0]
    const = lambda b: (0, 0)
    out = jax.ShapeDtypeStruct((n_tok, aug_w), _BF16)
    return pl.pallas_call(
        functools.partial(_decay_kernel, seq=seq, blk=blk, n_heads=n_heads),
        grid=(batch,),
        in_specs=[pl.BlockSpec((seq, LANES), lambda b: (b, 0))]
        + [_resident(c.shape, const) for c in consts],
        out_specs=[pl.BlockSpec((seq, aug_w), lambda b: (b, 0))] * 2,
        out_shape=[out, out],
        compiler_params=_params("arbitrary"),
        name="decay",
    )(logf, *consts)


def _decay_constants(n_heads, head_dim):
    assert DECAY_PARTS * n_heads <= LANES and 2 * DECAY_PARTS <= head_dim
    r = np.arange(CUMSUM_BLOCK)
    tri = (r[:, None] >= r[None, :]).astype(np.float32)
    width = (n_heads // 2) * LANES
    sq = np.zeros((LANES, width), np.float32)
    sk = np.zeros_like(sq)
    constq = np.zeros((1, width), np.float32)
    constk = np.zeros_like(constq)
    for h in range(n_heads):
        base = (h // 2) * LANES + (head_dim if h % 2 == 0 else 0)
        for part in range(DECAY_PARTS):
            sq[part * n_heads + h, base + part] = 1.0
            sk[part * n_heads + h, base + DECAY_PARTS + part] = -1.0
            constq[0, base + DECAY_PARTS + part] = 1.0
            constk[0, base + part] = 1.0
    return (jnp.asarray(tri, _BF16), jnp.asarray(sq, _BF16), jnp.asarray(sk, _BF16),
            jnp.asarray(constq), jnp.asarray(constk))


def _fox_kernel(q_ref, eq_ref, k_ref, ek_ref, v_ref, o_ref, *, seq, tq, tk, head_dim):
    lane = lax.broadcasted_iota(jnp.int32, (1, 2 * head_dim), 1)
    first = lane < head_dim
    pair_w = 2 * head_dim
    ones_col = jnp.where(lax.broadcasted_iota(jnp.int32, (tk, LANES), 1) == 0, 1.0, 0.0).astype(_BF16)

    def update(m, acc, q_rows, k_rows, v_rows, row_offset):
        s = _dot_nt(q_rows, k_rows)
        if row_offset is not None:
            rows = lax.broadcasted_iota(jnp.int32, s.shape, 0)
            cols = lax.broadcasted_iota(jnp.int32, s.shape, 1)
            s = jnp.where(rows + row_offset >= cols, s, MASK_VALUE)
        m_new = jnp.maximum(m, jnp.max(s, axis=1, keepdims=True))
        alpha = jnp.exp2(m - m_new)
        p = jnp.exp2(s - m_new).astype(_BF16)
        return m_new, alpha * acc + _dot(p, v_rows)

    n_q = seq // tq
    q_aug, state = [], []
    for qi in range(n_q):
        qs = slice(qi * tq, (qi + 1) * tq)
        q = q_ref[qs, :]
        eq = eq_ref[qs, :]
        q_aug.append((jnp.where(first, q, eq), jnp.where(first, eq, q)))
        state.append([(jnp.full((tq, 1), MASK_VALUE, _F32),
                       jnp.zeros((tq, pair_w + LANES), _F32)) for _ in range(2)])

    for j in range(n_q):
        ks = slice(j * tk, (j + 1) * tk)
        kc = k_ref[ks, :]
        v_aug = jnp.concatenate([v_ref[ks, :], ones_col], axis=1)
        ekc = ek_ref[ks, :]
        k_aug = (jnp.where(first, kc, ekc), jnp.where(first, ekc, kc))
        for qi in range(j, n_q):
            for e in range(2):
                m, acc = state[qi][e]
                state[qi][e] = update(m, acc, q_aug[qi][e], k_aug[e], v_aug,
                                      0 if j == qi else None)

    for qi in range(n_q):
        acc0, acc1 = state[qi][0][1], state[qi][1][1]
        o_ref[qi * tq:(qi + 1) * tq, :] = jnp.where(
            first, acc0[:, :pair_w] / acc0[:, pair_w:pair_w + 1],
            acc1[:, :pair_w] / acc1[:, pair_w:pair_w + 1]).astype(_BF16)


def _fox(qkv, eq, ek, *, batch, seq, n_heads, head_dim):
    n_tok = qkv.shape[0]
    fox_w = n_heads * head_dim
    n_pairs = n_heads // 2
    pair_w = 2 * head_dim
    tq, tk = ATTN_Q_TILE, ATTN_K_TILE
    assert tq == tk and pair_w == LANES
    return pl.pallas_call(
        functools.partial(_fox_kernel, seq=seq, tq=tq, tk=tk, head_dim=head_dim),
        grid=(batch, n_pairs),
        in_specs=[
            pl.BlockSpec((seq, pair_w), lambda b, p: (b, p)),
            pl.BlockSpec((seq, LANES), lambda b, p: (b, p)),
            pl.BlockSpec((seq, pair_w), lambda b, p: (b, n_pairs + p)),
            pl.BlockSpec((seq, LANES), lambda b, p: (b, p)),
            pl.BlockSpec((seq, pair_w), lambda b, p: (b, 2 * n_pairs + p)),
        ],
        out_specs=pl.BlockSpec((seq, pair_w), lambda b, p: (b, p)),
        out_shape=jax.ShapeDtypeStruct((n_tok, fox_w), _BF16),
        compiler_params=_params("arbitrary", "arbitrary"),
        name="fox",
    )(qkv, eq, qkv, ek, qkv)


def _post_kernel(a_ref, cc_ref, gate_ref, h_ref, wua_ref, wub_ref, wo_ref, gq_ref, wcq_ref,
                 cqg_ref, km_ref, vm_ref, wco_ref, *rest, n_heads, head_dim, n_staged):
    staged_src, (o_ref, *staged_dst, merged_s) = rest[:n_staged], rest[n_staged:]
    _cast_staged(staged_src, staged_dst)
    d_model = h_ref.shape[-1]
    scale = head_dim ** -0.5
    a = a_ref[...]
    cc = cc_ref[...]
    for c in range(d_model // MERGE_CHUNK):
        cs = slice(c * MERGE_CHUNK, (c + 1) * MERGE_CHUNK)
        up_a = _dot(a, wua_ref[:, cs])
        up_b = _dot(cc, wub_ref[:, cs])
        merged_s[:, cs] = (gate_ref[:, cs].astype(_F32) * up_a
                           + gate_ref[:, d_model + cs.start:d_model + cs.stop].astype(_F32) * up_b
                           ).astype(_BF16)
    h1 = h_ref[...] + _dot(merged_s[...], wo_ref[...])

    u = _rms_rows(h1, gq_ref[...]).astype(_BF16)
    qm = _dot(u, wcq_ref[...])
    heads = []
    for e in range(n_heads):
        sl = slice(e * head_dim, (e + 1) * head_dim)
        qe = qm[:, sl]
        ms = jnp.mean(qe * qe, axis=-1, keepdims=True)
        qn = (qe * lax.rsqrt(ms + EPS) * cqg_ref[...]).astype(_BF16)
        s = _dot_nt(qn, km_ref[:, sl]) * scale
        p = jnp.exp(s - jnp.max(s, axis=1, keepdims=True))
        den = jnp.sum(p, axis=1, keepdims=True)
        heads.append((_dot(p.astype(_BF16), vm_ref[:, sl]) / den).astype(_BF16))
    o_ref[...] = h1 + _dot(jnp.concatenate(heads, axis=1), wco_ref[...])


def _post(a_out, cc, gates, h, layer, w_up_a, w_up_b, w_o, gq, w_cq, cq_gain_l, kmem, vmem, w_co,
          staged, *, batch, seq):
    n_tok, d_model = h.shape
    tm = TOKEN_TILE
    nt = seq // tm
    fox_w = a_out.shape[1]
    conv_wd = cc.shape[1]
    mem_w = kmem.shape[-1]
    m_tok = kmem.shape[1] // batch
    head_dim = cq_gain_l.shape[-1]
    row = lambda b, j: (b * nt + j, 0)
    const2 = lambda b, j: (0, 0)
    st_in, st_out, st_shapes = _cast_staging(staged, batch * nt, lambda b, j: b * nt + j)
    return pl.pallas_call(
        functools.partial(_post_kernel, n_heads=mem_w // head_dim, head_dim=head_dim,
                          n_staged=len(staged)),
        grid=(batch, nt),
        in_specs=[
            pl.BlockSpec((tm, fox_w), row),
            pl.BlockSpec((tm, conv_wd), row),
            pl.BlockSpec((tm, 2 * d_model), row),
            pl.BlockSpec((tm, d_model), row),
            _resident(w_up_a.shape, const2),
            _resident(w_up_b.shape, const2),
            _resident(w_o.shape, const2),
            _resident((1, d_model), const2),
            _resident(w_cq.shape, const2),
            _resident((1, head_dim), const2),
            pl.BlockSpec((None, m_tok, mem_w), lambda b, j: (layer, b, 0)),
            pl.BlockSpec((None, m_tok, mem_w), lambda b, j: (layer, b, 0)),
            _resident(w_co.shape, const2),
        ] + st_in,
        out_specs=[pl.BlockSpec((tm, d_model), row)] + st_out,
        out_shape=[jax.ShapeDtypeStruct((n_tok, d_model), _F32)] + st_shapes,
        scratch_shapes=[pltpu.VMEM((tm, d_model), _BF16)],
        compiler_params=_params("arbitrary", "arbitrary"),
        name="post",
    )(a_out, cc, gates, h, w_up_a, w_up_b, w_o, gq, w_cq, cq_gain_l, kmem, vmem, w_co,
      *[s[0] for s in staged])


def _ffn_kernel(h_ref, g_ref, wgu_ref, wd_ref, *rest, d_ff, chunk, n_staged):
    staged_src, (o_ref, *staged_dst, act_s) = rest[:n_staged], rest[n_staged:]
    _cast_staged(staged_src, staged_dst)
    h = h_ref[...]
    u = _rms_rows(h, g_ref[...]).astype(_BF16)
    for c in range(d_ff // chunk):
        gate = _dot(u, wgu_ref[:, c * chunk:(c + 1) * chunk])
        up = _dot(u, wgu_ref[:, d_ff + c * chunk:d_ff + (c + 1) * chunk])
        act_s[:, c * chunk:(c + 1) * chunk] = (gate * jax.nn.sigmoid(gate) * up).astype(_BF16)
    o_ref[...] = h + _dot(act_s[...], wd_ref[...])


def _ffn(h, gain, w_gu, w_down, staged):
    n_tok, d_model = h.shape
    d_ff = w_down.shape[0]
    tm = TOKEN_TILE
    assert d_ff % FFN_CHUNK == 0
    st_in, st_out, st_shapes = _cast_staging(staged, n_tok // tm, lambda i: i)
    return pl.pallas_call(
        functools.partial(_ffn_kernel, d_ff=d_ff, chunk=FFN_CHUNK, n_staged=len(staged)),
        grid=(n_tok // tm,),
        in_specs=[
            pl.BlockSpec((tm, d_model), lambda i: (i, 0)),
            _resident((1, d_model), lambda i: (0, 0)),
            _resident(w_gu.shape, lambda i: (0, 0)),
            _resident(w_down.shape, lambda i: (0, 0)),
        ] + st_in,
        out_specs=[pl.BlockSpec((tm, d_model), lambda i: (i, 0))] + st_out,
        out_shape=[jax.ShapeDtypeStruct((n_tok, d_model), _F32)] + st_shapes,
        scratch_shapes=[pltpu.VMEM((tm, d_ff), _BF16)],
        compiler_params=_params("arbitrary"),
        name="ffn",
    )(h, gain, w_gu, w_down, *[s[0] for s in staged])


def kernel(x, mem, norm_mix, w_in, b_f, q_gain, k_gain, conv_w, w_up_a, w_up_b, w_o,
           norm_mem_q, norm_mem_kv, w_cq, w_ckv, cq_gain, ck_gain, w_co, norm_ffn, w_gu, w_down):
    batch, seq, d_model = x.shape
    n_layers = w_in.shape[0]
    head_dim = q_gain.shape[-1]
    n_heads = b_f.shape[-1]
    fox_w = n_heads * head_dim
    conv_wd = conv_w.shape[-1]
    gate_w = 2 * d_model
    assert seq % TOKEN_TILE == 0 and seq % ATTN_Q_TILE == 0 and seq % CUMSUM_BLOCK == 0
    assert w_in.shape[-1] == 3 * fox_w + n_heads + 3 * conv_wd + gate_w

    f0 = 3 * fox_w
    w_in_t = jnp.swapaxes(w_in, 1, 2)
    rest_rows = w_in_t.shape[1] - f0 - n_heads
    f_pad = LANES - DECAY_PARTS * n_heads
    w_f = jnp.pad(jnp.tile(w_in_t[:, f0:f0 + n_heads], (1, DECAY_PARTS, 1)),
                  ((0, 0), (0, f_pad), (0, 0))).astype(_BF16)
    bf_pad = jnp.pad(jnp.tile(b_f, (1, DECAY_PARTS)), ((0, 0), (0, f_pad)))

    def mixer_sources(layer):
        return [(w_in_t, layer, 0, f0), (w_in_t, layer, f0 + n_heads, rest_rows)] + [
            (w, layer, 0, w.shape[1]) for w in (w_up_a, w_up_b, w_o, w_cq, w_co)]

    qg = jnp.tile(q_gain, (1, n_heads)) * (head_dim ** -0.5 * LOG2E)
    kg = jnp.tile(k_gain, (1, n_heads))

    decay_consts = _decay_constants(n_heads, head_dim)

    kmem, vmem, *mixer_w = _memkv(mem, norm_mem_kv, w_ckv, ck_gain, mixer_sources(0))

    h = x.reshape(batch * seq, d_model)
    dims = (fox_w, conv_wd, gate_w, head_dim)
    for l in range(n_layers):
        w_qkv, w_rest, w_up_a_bf, w_up_b_bf, w_o_bf, w_cq_bf, w_co_bf = mixer_w
        qkv, logf, cc, gates = _inproj(h, norm_mix[l][None], w_qkv, w_rest, w_f[l], bf_pad[l][None],
                                       qg[l][None], kg[l][None], conv_w[l],
                                       batch=batch, seq=seq, dims=dims)
        eq, ek = _decay(logf, decay_consts, batch=batch, seq=seq, n_heads=n_heads)
        a_out = _fox(qkv, eq, ek, batch=batch, seq=seq, n_heads=n_heads, head_dim=head_dim)
        ffn_sources = [(w_gu, l, 0, w_gu.shape[1]), (w_down, l, 0, w_down.shape[1])]
        h, w_gu_bf, w_down_bf = _post(
            a_out, cc, gates, h, l, w_up_a_bf, w_up_b_bf, w_o_bf, norm_mem_q[l][None], w_cq_bf,
            cq_gain[l][None], kmem, vmem, w_co_bf, ffn_sources, batch=batch, seq=seq)
        next_sources = mixer_sources(l + 1) if l + 1 < n_layers else []
        h, *mixer_w = _ffn(h, norm_ffn[l][None], w_gu_bf, w_down_bf, next_sources)
    return h.reshape(batch, seq, d_model)
```

```python
import functools

import jax
import jax.numpy as jnp
import numpy as np
from jax import lax
from jax.experimental import pallas as pl
from jax.experimental.pallas import tpu as pltpu

_BF16 = jnp.bfloat16
_F32 = jnp.float32

EPS = 1e-6
LOG2E = 1.4426950408889634
LANES = 128
SUBLANES = 8
VMEM_LIMIT_BYTES = 56 * 1024 * 1024
MASK_VALUE = -1e30
TOKEN_TILE = 1024
MEMKV_SPLIT = 2
MERGE_CHUNK = 256
ATTN_Q_TILE = 512
ATTN_K_TILE = 512
ATTN_PAIRS_PER_STEP = 2
CUMSUM_BLOCK = 256
DECAY_PARTS = 3
FFN_CHUNK = 256


def _dot(a, b):
    return jnp.dot(a, b, preferred_element_type=_F32)


def _dot_nt(a, b):
    return lax.dot_general(a, b, (((1,), (1,)), ((), ())), preferred_element_type=_F32)


def _rms_rows(x, gain):
    ms = jnp.mean(x * x, axis=-1, keepdims=True)
    return x * lax.rsqrt(ms + EPS) * gain


def _split3(x):
    hi = x.astype(_BF16)
    r1 = x - hi.astype(_F32)
    mid = r1.astype(_BF16)
    lo = (r1 - mid.astype(_F32)).astype(_BF16)
    return hi, mid, lo


def _params(*sem):
    return pltpu.CompilerParams(dimension_semantics=sem, vmem_limit_bytes=VMEM_LIMIT_BYTES)


def _resident(block_shape, index_map):
    return pl.BlockSpec(block_shape, index_map, pipeline_mode=pl.Buffered(1))


def _cast_staging(sources, n_steps, step_of):
    in_specs, out_specs, out_shapes = [], [], []
    for arr, layer, first_row, n_rows in sources:
        cols = arr.shape[-1]
        slab = n_rows // n_steps
        assert slab * n_steps == n_rows and slab % (2 * SUBLANES) == 0 and first_row % SUBLANES == 0
        in_specs.append(pl.BlockSpec(
            (None, pl.Element(slab), pl.Element(cols)),
            lambda *g, layer=layer, first_row=first_row, slab=slab:
                (layer, pl.multiple_of(first_row + slab * step_of(*g), SUBLANES), 0)))
        out_specs.append(pl.BlockSpec((slab, cols), lambda *g: (step_of(*g), 0)))
        out_shapes.append(jax.ShapeDtypeStruct((n_rows, cols), _BF16))
    return in_specs, out_specs, out_shapes


def _cast_staged(src_refs, dst_refs):
    for src, dst in zip(src_refs, dst_refs, strict=True):
        dst[...] = src[...].astype(_BF16)


def _memkv_kernel(mem_ref, g_ref, w_ref, ckg_ref, *rest, n_heads, head_dim, n_staged):
    staged_src, (k_ref, v_ref, *staged_dst) = rest[:n_staged], rest[n_staged:]
    _cast_staged(staged_src, staged_dst)
    u = _rms_rows(mem_ref[...], g_ref[...]).astype(_BF16)
    kv = _dot(u, w_ref[...].astype(_BF16))
    width = n_heads * head_dim
    for e in range(n_heads):
        ke = kv[:, e * head_dim:(e + 1) * head_dim]
        ms = jnp.mean(ke * ke, axis=-1, keepdims=True)
        k_ref[:, e * head_dim:(e + 1) * head_dim] = (
            ke * lax.rsqrt(ms + EPS) * ckg_ref[...]).astype(_BF16)
    v_ref[...] = kv[:, width:].astype(_BF16)


def _memkv(mem, norm_mem_kv, w_ckv, ck_gain, staged):
    n_layers, d_model, two_w = w_ckv.shape
    batch, m_tok, _ = mem.shape
    rows = batch * m_tok
    blk = rows // MEMKV_SPLIT
    width = two_w // 2
    head_dim = ck_gain.shape[-1]
    n_heads = width // head_dim
    out = jax.ShapeDtypeStruct((n_layers, rows, width), _BF16)
    st_in, st_out, st_shapes = _cast_staging(staged, n_layers * MEMKV_SPLIT,
                                             lambda l, s: l * MEMKV_SPLIT + s)
    return pl.pallas_call(
        functools.partial(_memkv_kernel, n_heads=n_heads, head_dim=head_dim,
                          n_staged=len(staged)),
        grid=(n_layers, MEMKV_SPLIT),
        in_specs=[
            pl.BlockSpec((blk, d_model), lambda l, s: (s, 0)),
            pl.BlockSpec((None, 1, d_model), lambda l, s: (l, 0, 0)),
            pl.BlockSpec((None, d_model, two_w), lambda l, s: (l, 0, 0)),
            pl.BlockSpec((None, 1, head_dim), lambda l, s: (l, 0, 0)),
        ] + st_in,
        out_specs=[
            pl.BlockSpec((None, blk, width), lambda l, s: (l, s, 0)),
            pl.BlockSpec((None, blk, width), lambda l, s: (l, s, 0)),
        ] + st_out,
        out_shape=[out, out] + st_shapes,
        compiler_params=_params("arbitrary", "arbitrary"),
        name="memkv",
    )(mem.reshape(rows, d_model), norm_mem_kv[:, None, :], w_ckv, ck_gain[:, None, :],
      *[s[0] for s in staged])


def _inproj_kernel(h_ref, g_ref, wqkv_ref, wrest_ref, wf_ref, bf_ref, qg_ref, kg_ref, cw_ref,
                   qkv_ref, lf_ref, cc_ref, gate_ref, tail_s,
                   *, tm, fox_w, conv_w, gate_w, head_dim):
    j = pl.program_id(1)
    u = _rms_rows(h_ref[...], g_ref[...]).astype(_BF16)

    assert 2 * head_dim == LANES
    first = lax.broadcasted_iota(jnp.int32, (1, LANES), 1) < head_dim
    for idx, gain_ref in ((0, qg_ref), (1, kg_ref)):
        a = _dot_nt(u, wqkv_ref[idx * fox_w:(idx + 1) * fox_w, :])
        for c in range(fox_w // LANES):
            cs = slice(c * LANES, (c + 1) * LANES)
            ac = a[:, cs]
            sq = ac * ac
            lo = jnp.sum(jnp.where(first, sq, 0.0), axis=1, keepdims=True)
            hi = jnp.sum(jnp.where(first, 0.0, sq), axis=1, keepdims=True)
            ss = jnp.where(first, lo, hi)
            qkv_ref[:, idx * fox_w + cs.start:idx * fox_w + cs.stop] = (
                ac * lax.rsqrt(ss * (1.0 / head_dim) + EPS) * gain_ref[:, cs]).astype(_BF16)
    qkv_ref[:, 2 * fox_w:3 * fox_w] = _dot_nt(u, wqkv_ref[2 * fox_w:3 * fox_w, :]).astype(_BF16)

    @pl.when(j == 0)
    def _():
        tail_s[...] = jnp.zeros(tail_s.shape, _F32)

    n_taps = cw_ref.shape[0]
    hr = tail_s.shape[0]
    assert n_taps - 1 <= hr and conv_w % (2 * LANES) == 0
    sub = lax.broadcasted_iota(jnp.int32, (hr, LANES), 0)
    for cp in range(conv_w // (2 * LANES)):
        gate_b = _dot_nt(u, wrest_ref[conv_w + cp * 2 * LANES:conv_w + (cp + 1) * 2 * LANES, :])
        for half in range(2):
            cs = slice((2 * cp + half) * LANES, (2 * cp + half + 1) * LANES)
            w_pair = jnp.concatenate(
                [wrest_ref[cs, :], wrest_ref[2 * conv_w + cs.start:2 * conv_w + cs.stop, :]], axis=0)
            r = _dot_nt(u, w_pair)
            zc = r[:, :LANES] * r[:, LANES:]
            y = cw_ref[n_taps - 1:n_taps, cs] * zc
            head = zc[0:hr, :]
            y_head = cw_ref[n_taps - 1:n_taps, cs] * head
            prev = tail_s[:, cs]
            for d in range(1, n_taps):
                w_d = cw_ref[n_taps - 1 - d:n_taps - d, cs]
                y = y + w_d * pltpu.roll(zc, d, axis=0)
                y_head = y_head + w_d * jnp.where(sub < d, pltpu.roll(prev, d, axis=0),
                                                  pltpu.roll(head, d, axis=0))
            gb = gate_b[:, half * LANES:(half + 1) * LANES]
            cc_ref[0:hr, cs] = (gb[0:hr, :] * y_head).astype(_BF16)
            cc_ref[hr:tm, cs] = (gb[hr:tm, :] * y[hr:tm, :]).astype(_BF16)
            tail_s[:, cs] = zc[tm - hr:tm, :]

    o = 3 * conv_w
    step = conv_w
    for i in range(gate_w // step):
        a = _dot_nt(u, wrest_ref[o + i * step:o + (i + 1) * step, :])
        gate_ref[:, i * step:(i + 1) * step] = jax.nn.sigmoid(a).astype(_BF16)

    a = _dot_nt(u, wf_ref[...]) + bf_ref[...]
    lf_ref[...] = jnp.minimum(a, 0.0) - jnp.log1p(jnp.exp(-jnp.abs(a)))


def _inproj(h, gain, w_qkv, w_rest, w_f, bf_pad, qg, kg, conv_w_l, *, batch, seq, dims):
    fox_w, conv_wd, gate_w, head_dim = dims
    n_tok, d_model = h.shape
    tm = TOKEN_TILE
    nt = seq // tm
    row = lambda b, j: (b * nt + j, 0)
    const2 = lambda b, j: (0, 0)
    return pl.pallas_call(
        functools.partial(_inproj_kernel, tm=tm, fox_w=fox_w, conv_w=conv_wd, gate_w=gate_w,
                          head_dim=head_dim),
        grid=(batch, nt),
        in_specs=[
            pl.BlockSpec((tm, d_model), row),
            _resident((1, d_model), const2),
            _resident(w_qkv.shape, const2),
            _resident(w_rest.shape, const2),
            _resident(w_f.shape, const2),
            _resident((1, LANES), const2),
            _resident((1, fox_w), const2),
            _resident((1, fox_w), const2),
            _resident(conv_w_l.shape, const2),
        ],
        out_specs=[
            pl.BlockSpec((tm, 3 * fox_w), row),
            pl.BlockSpec((tm, LANES), row),
            pl.BlockSpec((tm, conv_wd), row),
            pl.BlockSpec((tm, gate_w), row),
        ],
        out_shape=[
            jax.ShapeDtypeStruct((n_tok, 3 * fox_w), _BF16),
            jax.ShapeDtypeStruct((n_tok, LANES), _F32),
            jax.ShapeDtypeStruct((n_tok, conv_wd), _BF16),
            jax.ShapeDtypeStruct((n_tok, gate_w), _BF16),
        ],
        scratch_shapes=[pltpu.VMEM((2 * SUBLANES, conv_wd), _F32)],
        compiler_params=_params("arbitrary", "arbitrary"),
        name="inproj",
    )(h, gain, w_qkv, w_rest, w_f, bf_pad, qg, kg, conv_w_l)


def _decay_kernel(lf_ref, tri_ref, sq_ref, sk_ref, constq_ref, constk_ref, eq_ref, ek_ref,
                  *, seq, blk, n_heads):
    tri = tri_ref[...]
    lane = lax.broadcasted_iota(jnp.int32, (1, LANES), 1)
    carry = jnp.zeros((1, LANES), _F32)
    for i in range(seq // blk):
        rows = slice(i * blk, (i + 1) * blk)
        hi, mid, lo = _split3(lf_ref[rows, :])
        c = (_dot(tri, lo) + _dot(tri, mid)) + _dot(tri, hi) + carry
        carry = c[blk - 1:blk, :]
        hi, mid, lo = _split3(c * LOG2E)
        parts = jnp.where(lane < n_heads, hi, jnp.where(lane < 2 * n_heads, mid, lo))
        eq_ref[rows, :] = (_dot(parts, sq_ref[...]) + constq_ref[...]).astype(_BF16)
        ek_ref[rows, :] = (_dot(parts, sk_ref[...]) + constk_ref[...]).astype(_BF16)


def _decay(logf, consts, *, batch, seq, n_heads):
    tri, sq = consts[0], consts[1]
    blk = tri.shape[0]
    aug_w = sq.shape[1]
    n_tok = logf.shape[0]
    const = lambda b: (0, 0)
    out = jax.ShapeDtypeStruct((n_tok, aug_w), _BF16)
    return pl.pallas_call(
        functools.partial(_decay_kernel, seq=seq, blk=blk, n_heads=n_heads),
        grid=(batch,),
        in_specs=[pl.BlockSpec((seq, LANES), lambda b: (b, 0))]
        + [_resident(c.shape, const) for c in consts],
        out_specs=[pl.BlockSpec((seq, aug_w), lambda b: (b, 0))] * 2,
        out_shape=[out, out],
        compiler_params=_params("arbitrary"),
        name="decay",
    )(logf, *consts)


def _decay_constants(n_heads, head_dim):
    assert DECAY_PARTS * n_heads <= LANES and 2 * DECAY_PARTS <= head_dim
    r = np.arange(CUMSUM_BLOCK)
    tri = (r[:, None] >= r[None, :]).astype(np.float32)
    width = (n_heads // 2) * LANES
    sq = np.zeros((LANES, width), np.float32)
    sk = np.zeros_like(sq)
    constq = np.zeros((1, width), np.float32)
    constk = np.zeros_like(constq)
    for h in range(n_heads):
        base = (h // 2) * LANES + (head_dim if h % 2 == 0 else 0)
        for part in range(DECAY_PARTS):
            sq[part * n_heads + h, base + part] = 1.0
            sk[part * n_heads + h, base + DECAY_PARTS + part] = -1.0
            constq[0, base + DECAY_PARTS + part] = 1.0
            constk[0, base + part] = 1.0
    return (jnp.asarray(tri, _BF16), jnp.asarray(sq, _BF16), jnp.asarray(sk, _BF16),
            jnp.asarray(constq), jnp.asarray(constk))


def _fox_kernel(q_ref, eq_ref, k_ref, ek_ref, v_ref, o_ref, *, seq, tq, tk, head_dim):
    lane = lax.broadcasted_iota(jnp.int32, (1, 2 * head_dim), 1)
    first = lane < head_dim
    pair_w = 2 * head_dim
    ones_col = jnp.where(lax.broadcasted_iota(jnp.int32, (tk, LANES), 1) == 0, 1.0, 0.0).astype(_BF16)

    def update(m, acc, q_rows, k_rows, v_rows, row_offset):
        s = _dot_nt(q_rows, k_rows)
        if row_offset is not None:
            rows = lax.broadcasted_iota(jnp.int32, s.shape, 0)
            cols = lax.broadcasted_iota(jnp.int32, s.shape, 1)
            s = jnp.where(rows + row_offset >= cols, s, MASK_VALUE)
        m_new = jnp.maximum(m, jnp.max(s, axis=1, keepdims=True))
        alpha = jnp.exp2(m - m_new)
        p = jnp.exp2(s - m_new).astype(_BF16)
        return m_new, alpha * acc + _dot(p, v_rows)

    n_q = seq // tq
    for pr in range(q_ref.shape[1] // pair_w):
        pc = slice(pr * pair_w, (pr + 1) * pair_w)
        q_aug, state = [], []
        for qi in range(n_q):
            qs = slice(qi * tq, (qi + 1) * tq)
            q = q_ref[qs, pc]
            eq = eq_ref[qs, pc]
            q_aug.append((jnp.where(first, q, eq), jnp.where(first, eq, q)))
            state.append([(jnp.full((tq, 1), MASK_VALUE, _F32),
                           jnp.zeros((tq, pair_w + LANES), _F32)) for _ in range(2)])

        for j in range(n_q):
            ks = slice(j * tk, (j + 1) * tk)
            kc = k_ref[ks, pc]
            v_aug = jnp.concatenate([v_ref[ks, pc], ones_col], axis=1)
            ekc = ek_ref[ks, pc]
            k_aug = (jnp.where(first, kc, ekc), jnp.where(first, ekc, kc))
            for qi in range(j, n_q):
                for e in range(2):
                    m, acc = state[qi][e]
                    state[qi][e] = update(m, acc, q_aug[qi][e], k_aug[e], v_aug,
                                          0 if j == qi else None)

        for qi in range(n_q):
            acc0, acc1 = state[qi][0][1], state[qi][1][1]
            o_ref[qi * tq:(qi + 1) * tq, pc] = jnp.where(
                first, acc0[:, :pair_w] / acc0[:, pair_w:pair_w + 1],
                acc1[:, :pair_w] / acc1[:, pair_w:pair_w + 1]).astype(_BF16)


def _fox(qkv, eq, ek, *, batch, seq, n_heads, head_dim):
    n_tok = qkv.shape[0]
    fox_w = n_heads * head_dim
    n_pairs = n_heads // 2
    pair_w = 2 * head_dim
    tq, tk = ATTN_Q_TILE, ATTN_K_TILE
    gw = ATTN_PAIRS_PER_STEP * pair_w
    n_groups = n_pairs // ATTN_PAIRS_PER_STEP
    assert tq == tk and pair_w == LANES and n_pairs % ATTN_PAIRS_PER_STEP == 0
    return pl.pallas_call(
        functools.partial(_fox_kernel, seq=seq, tq=tq, tk=tk, head_dim=head_dim),
        grid=(batch, n_groups),
        in_specs=[
            pl.BlockSpec((seq, gw), lambda b, p: (b, p)),
            pl.BlockSpec((seq, gw), lambda b, p: (b, p)),
            pl.BlockSpec((seq, gw), lambda b, p: (b, n_groups + p)),
            pl.BlockSpec((seq, gw), lambda b, p: (b, p)),
            pl.BlockSpec((seq, gw), lambda b, p: (b, 2 * n_groups + p)),
        ],
        out_specs=pl.BlockSpec((seq, gw), lambda b, p: (b, p)),
        out_shape=jax.ShapeDtypeStruct((n_tok, fox_w), _BF16),
        compiler_params=_params("arbitrary", "arbitrary"),
        name="fox",
    )(qkv, eq, qkv, ek, qkv)


def _post_kernel(a_ref, cc_ref, gate_ref, h_ref, wua_ref, wub_ref, wo_ref, gq_ref, wcq_ref,
                 cqg_ref, km_ref, vm_ref, wco_ref, *rest, n_heads, head_dim, n_staged):
    staged_src, (o_ref, *staged_dst, merged_s) = rest[:n_staged], rest[n_staged:]
    _cast_staged(staged_src, staged_dst)
    d_model = h_ref.shape[-1]
    scale = head_dim ** -0.5
    a = a_ref[...]
    cc = cc_ref[...]
    for c in range(d_model // MERGE_CHUNK):
        cs = slice(c * MERGE_CHUNK, (c + 1) * MERGE_CHUNK)
        up_a = _dot(a, wua_ref[:, cs])
        up_b = _dot(cc, wub_ref[:, cs])
        merged_s[:, cs] = (gate_ref[:, cs].astype(_F32) * up_a
                           + gate_ref[:, d_model + cs.start:d_model + cs.stop].astype(_F32) * up_b
                           ).astype(_BF16)
    h1 = h_ref[...] + _dot(merged_s[...], wo_ref[...])

    u = _rms_rows(h1, gq_ref[...]).astype(_BF16)
    qm = _dot(u, wcq_ref[...])
    heads = []
    for e in range(n_heads):
        sl = slice(e * head_dim, (e + 1) * head_dim)
        qe = qm[:, sl]
        ms = jnp.mean(qe * qe, axis=-1, keepdims=True)
        qn = (qe * lax.rsqrt(ms + EPS) * cqg_ref[...]).astype(_BF16)
        s = _dot_nt(qn, km_ref[:, sl]) * scale
        p = jnp.exp(s - jnp.max(s, axis=1, keepdims=True))
        den = jnp.sum(p, axis=1, keepdims=True)
        heads.append((_dot(p.astype(_BF16), vm_ref[:, sl]) / den).astype(_BF16))
    o_ref[...] = h1 + _dot(jnp.concatenate(heads, axis=1), wco_ref[...])


def _post(a_out, cc, gates, h, layer, w_up_a, w_up_b, w_o, gq, w_cq, cq_gain_l, kmem, vmem, w_co,
          staged, *, batch, seq):
    n_tok, d_model = h.shape
    tm = TOKEN_TILE
    nt = seq // tm
    fox_w = a_out.shape[1]
    conv_wd = cc.shape[1]
    mem_w = kmem.shape[-1]
    m_tok = kmem.shape[1] // batch
    head_dim = cq_gain_l.shape[-1]
    row = lambda b, j: (b * nt + j, 0)
    const2 = lambda b, j: (0, 0)
    st_in, st_out, st_shapes = _cast_staging(staged, batch * nt, lambda b, j: b * nt + j)
    return pl.pallas_call(
        functools.partial(_post_kernel, n_heads=mem_w // head_dim, head_dim=head_dim,
                          n_staged=len(staged)),
        grid=(batch, nt),
        in_specs=[
            pl.BlockSpec((tm, fox_w), row),
            pl.BlockSpec((tm, conv_wd), row),
            pl.BlockSpec((tm, 2 * d_model), row),
            pl.BlockSpec((tm, d_model), row),
            _resident(w_up_a.shape, const2),
            _resident(w_up_b.shape, const2),
            _resident(w_o.shape, const2),
            _resident((1, d_model), const2),
            _resident(w_cq.shape, const2),
            _resident((1, head_dim), const2),
            pl.BlockSpec((None, m_tok, mem_w), lambda b, j: (layer, b, 0)),
            pl.BlockSpec((None, m_tok, mem_w), lambda b, j: (layer, b, 0)),
            _resident(w_co.shape, const2),
        ] + st_in,
        out_specs=[pl.BlockSpec((tm, d_model), row)] + st_out,
        out_shape=[jax.ShapeDtypeStruct((n_tok, d_model), _F32)] + st_shapes,
        scratch_shapes=[pltpu.VMEM((tm, d_model), _BF16)],
        compiler_params=_params("arbitrary", "arbitrary"),
        name="post",
    )(a_out, cc, gates, h, w_up_a, w_up_b, w_o, gq, w_cq, cq_gain_l, kmem, vmem, w_co,
      *[s[0] for s in staged])


def _ffn_kernel(h_ref, g_ref, wgu_ref, wd_ref, *rest, d_ff, chunk, n_staged):
    staged_src, (o_ref, *staged_dst, act_s) = rest[:n_staged], rest[n_staged:]
    _cast_staged(staged_src, staged_dst)
    h = h_ref[...]
    u = _rms_rows(h, g_ref[...]).astype(_BF16)
    for c in range(d_ff // chunk):
        gate = _dot(u, wgu_ref[:, c * chunk:(c + 1) * chunk])
        up = _dot(u, wgu_ref[:, d_ff + c * chunk:d_ff + (c + 1) * chunk])
        act_s[:, c * chunk:(c + 1) * chunk] = (gate * jax.nn.sigmoid(gate) * up).astype(_BF16)
    o_ref[...] = h + _dot(act_s[...], wd_ref[...])


def _ffn(h, gain, w_gu, w_down, staged):
    n_tok, d_model = h.shape
    d_ff = w_down.shape[0]
    tm = TOKEN_TILE
    assert d_ff % FFN_CHUNK == 0
    st_in, st_out, st_shapes = _cast_staging(staged, n_tok // tm, lambda i: i)
    return pl.pallas_call(
        functools.partial(_ffn_kernel, d_ff=d_ff, chunk=FFN_CHUNK, n_staged=len(staged)),
        grid=(n_tok // tm,),
        in_specs=[
            pl.BlockSpec((tm, d_model), lambda i: (i, 0)),
            _resident((1, d_model), lambda i: (0, 0)),
            _resident(w_gu.shape, lambda i: (0, 0)),
            _resident(w_down.shape, lambda i: (0, 0)),
        ] + st_in,
        out_specs=[pl.BlockSpec((tm, d_model), lambda i: (i, 0))] + st_out,
        out_shape=[jax.ShapeDtypeStruct((n_tok, d_model), _F32)] + st_shapes,
        scratch_shapes=[pltpu.VMEM((tm, d_ff), _BF16)],
        compiler_params=_params("arbitrary"),
        name="ffn",
    )(h, gain, w_gu, w_down, *[s[0] for s in staged])


def kernel(x, mem, norm_mix, w_in, b_f, q_gain, k_gain, conv_w, w_up_a, w_up_b, w_o,
           norm_mem_q, norm_mem_kv, w_cq, w_ckv, cq_gain, ck_gain, w_co, norm_ffn, w_gu, w_down):
    batch, seq, d_model = x.shape
    n_layers = w_in.shape[0]
    head_dim = q_gain.shape[-1]
    n_heads = b_f.shape[-1]
    fox_w = n_heads * head_dim
    conv_wd = conv_w.shape[-1]
    gate_w = 2 * d_model
    assert seq % TOKEN_TILE == 0 and seq % ATTN_Q_TILE == 0 and seq % CUMSUM_BLOCK == 0
    assert w_in.shape[-1] == 3 * fox_w + n_heads + 3 * conv_wd + gate_w

    f0 = 3 * fox_w
    w_in_t = jnp.swapaxes(w_in, 1, 2)
    rest_rows = w_in_t.shape[1] - f0 - n_heads
    f_pad = LANES - DECAY_PARTS * n_heads
    w_f = jnp.pad(jnp.tile(w_in_t[:, f0:f0 + n_heads], (1, DECAY_PARTS, 1)),
                  ((0, 0), (0, f_pad), (0, 0))).astype(_BF16)
    bf_pad = jnp.pad(jnp.tile(b_f, (1, DECAY_PARTS)), ((0, 0), (0, f_pad)))

    def mixer_sources(layer):
        return [(w_in_t, layer, 0, f0), (w_in_t, layer, f0 + n_heads, rest_rows)] + [
            (w, layer, 0, w.shape[1]) for w in (w_up_a, w_up_b, w_o, w_cq, w_co)]

    qg = jnp.tile(q_gain, (1, n_heads)) * (head_dim ** -0.5 * LOG2E)
    kg = jnp.tile(k_gain, (1, n_heads))

    decay_consts = _decay_constants(n_heads, head_dim)

    kmem, vmem, *mixer_w = _memkv(mem, norm_mem_kv, w_ckv, ck_gain, mixer_sources(0))

    h = x.reshape(batch * seq, d_model)
    dims = (fox_w, conv_wd, gate_w, head_dim)
    for l in range(n_layers):
        w_qkv, w_rest, w_up_a_bf, w_up_b_bf, w_o_bf, w_cq_bf, w_co_bf = mixer_w
        qkv, logf, cc, gates = _inproj(h, norm_mix[l][None], w_qkv, w_rest, w_f[l], bf_pad[l][None],
                                       qg[l][None], kg[l][None], conv_w[l],
                                       batch=batch, seq=seq, dims=dims)
        eq, ek = _decay(logf, decay_consts, batch=batch, seq=seq, n_heads=n_heads)
        a_out = _fox(qkv, eq, ek, batch=batch, seq=seq, n_heads=n_heads, head_dim=head_dim)
        ffn_sources = [(w_gu, l, 0, w_gu.shape[1]), (w_down, l, 0, w_down.shape[1])]
        h, w_gu_bf, w_down_bf = _post(
            a_out, cc, gates, h, l, w_up_a_bf, w_up_b_bf, w_o_bf, norm_mem_q[l][None], w_cq_bf,
            cq_gain[l][None], kmem, vmem, w_co_bf, ffn_sources, batch=batch, seq=seq)
        next_sources = mixer_sources(l + 1) if l + 1 < n_layers else []
        h, *mixer_w = _ffn(h, norm_ffn[l][None], w_gu_bf, w_down_bf, next_sources)
    return h.reshape(batch, seq, d_model)
```

```python
import functools

import jax
import jax.numpy as jnp
import numpy as np
from jax import lax
from jax.experimental import pallas as pl
from jax.experimental.pallas import tpu as pltpu

_BF16 = jnp.bfloat16
_F32 = jnp.float32

EPS = 1e-6
LOG2E = 1.4426950408889634
LANES = 128
SUBLANES = 8
VMEM_LIMIT_BYTES = 56 * 1024 * 1024
MASK_VALUE = -1e30
TOKEN_TILE = 1024
MEMKV_SPLIT = 2
MERGE_CHUNK = 256
ATTN_Q_TILE = 512
ATTN_K_TILE = 512
ATTN_PAIRS_PER_STEP = 2
CUMSUM_BLOCK = 256
DECAY_PARTS = 3
FFN_CHUNK = 256


def _dot(a, b):
    return jnp.dot(a, b, preferred_element_type=_F32)


def _dot_nt(a, b):
    return lax.dot_general(a, b, (((1,), (1,)), ((), ())), preferred_element_type=_F32)


def _rms_rows(x, gain):
    ms = jnp.mean(x * x, axis=-1, keepdims=True)
    return x * lax.rsqrt(ms + EPS) * gain


def _split3(x):
    hi = x.astype(_BF16)
    r1 = x - hi.astype(_F32)
    mid = r1.astype(_BF16)
    lo = (r1 - mid.astype(_F32)).astype(_BF16)
    return hi, mid, lo


def _params(*sem):
    return pltpu.CompilerParams(dimension_semantics=sem, vmem_limit_bytes=VMEM_LIMIT_BYTES)


def _resident(block_shape, index_map):
    return pl.BlockSpec(block_shape, index_map, pipeline_mode=pl.Buffered(1))


def _cast_staging(sources, n_steps, step_of):
    in_specs, out_specs, out_shapes = [], [], []
    for arr, layer, first_row, n_rows in sources:
        cols = arr.shape[-1]
        slab = n_rows // n_steps
        assert slab * n_steps == n_rows and slab % (2 * SUBLANES) == 0 and first_row % SUBLANES == 0
        in_specs.append(pl.BlockSpec(
            (None, pl.Element(slab), pl.Element(cols)),
            lambda *g, layer=layer, first_row=first_row, slab=slab:
                (layer, pl.multiple_of(first_row + slab * step_of(*g), SUBLANES), 0)))
        out_specs.append(pl.BlockSpec((slab, cols), lambda *g: (step_of(*g), 0)))
        out_shapes.append(jax.ShapeDtypeStruct((n_rows, cols), _BF16))
    return in_specs, out_specs, out_shapes


def _cast_staged(src_refs, dst_refs):
    for src, dst in zip(src_refs, dst_refs, strict=True):
        dst[...] = src[...].astype(_BF16)


def _memkv_kernel(mem_ref, g_ref, w_ref, ckg_ref, *rest, n_heads, head_dim, n_staged):
    staged_src, (k_ref, v_ref, *staged_dst) = rest[:n_staged], rest[n_staged:]
    _cast_staged(staged_src, staged_dst)
    u = _rms_rows(mem_ref[...], g_ref[...]).astype(_BF16)
    kv = _dot(u, w_ref[...].astype(_BF16))
    width = n_heads * head_dim
    for e in range(n_heads):
        ke = kv[:, e * head_dim:(e + 1) * head_dim]
        ms = jnp.mean(ke * ke, axis=-1, keepdims=True)
        k_ref[:, e * head_dim:(e + 1) * head_dim] = (
            ke * lax.rsqrt(ms + EPS) * ckg_ref[...]).astype(_BF16)
    v_ref[...] = kv[:, width:].astype(_BF16)


def _memkv(mem, norm_mem_kv, w_ckv, ck_gain, staged):
    n_layers, d_model, two_w = w_ckv.shape
    batch, m_tok, _ = mem.shape
    rows = batch * m_tok
    blk = rows // MEMKV_SPLIT
    width = two_w // 2
    head_dim = ck_gain.shape[-1]
    n_heads = width // head_dim
    out = jax.ShapeDtypeStruct((n_layers, rows, width), _BF16)
    st_in, st_out, st_shapes = _cast_staging(staged, n_layers * MEMKV_SPLIT,
                                             lambda l, s: l * MEMKV_SPLIT + s)
    return pl.pallas_call(
        functools.partial(_memkv_kernel, n_heads=n_heads, head_dim=head_dim,
                          n_staged=len(staged)),
        grid=(n_layers, MEMKV_SPLIT),
        in_specs=[
            pl.BlockSpec((blk, d_model), lambda l, s: (s, 0)),
            pl.BlockSpec((None, 1, d_model), lambda l, s: (l, 0, 0)),
            pl.BlockSpec((None, d_model, two_w), lambda l, s: (l, 0, 0)),
            pl.BlockSpec((None, 1, head_dim), lambda l, s: (l, 0, 0)),
        ] + st_in,
        out_specs=[
            pl.BlockSpec((None, blk, width), lambda l, s: (l, s, 0)),
            pl.BlockSpec((None, blk, width), lambda l, s: (l, s, 0)),
        ] + st_out,
        out_shape=[out, out] + st_shapes,
        compiler_params=_params("arbitrary", "arbitrary"),
        name="memkv",
    )(mem.reshape(rows, d_model), norm_mem_kv[:, None, :], w_ckv, ck_gain[:, None, :],
      *[s[0] for s in staged])


def _inproj_kernel(h_ref, g_ref, wqkv_ref, wrest_ref, wf_ref, bf_ref, qg_ref, kg_ref, cw_ref,
                   qkv_ref, lf_ref, cc_ref, gate_ref, tail_s,
                   *, tm, fox_w, conv_w, gate_w, head_dim):
    j = pl.program_id(1)
    u = _rms_rows(h_ref[...], g_ref[...]).astype(_BF16)

    assert 2 * head_dim == LANES
    first = lax.broadcasted_iota(jnp.int32, (1, LANES), 1) < head_dim
    for idx, gain_ref in ((0, qg_ref), (1, kg_ref)):
        a = _dot_nt(u, wqkv_ref[idx * fox_w:(idx + 1) * fox_w, :])
        for c in range(fox_w // LANES):
            cs = slice(c * LANES, (c + 1) * LANES)
            ac = a[:, cs]
            sq = ac * ac
            lo = jnp.sum(jnp.where(first, sq, 0.0), axis=1, keepdims=True)
            hi = jnp.sum(jnp.where(first, 0.0, sq), axis=1, keepdims=True)
            ss = jnp.where(first, lo, hi)
            qkv_ref[:, idx * fox_w + cs.start:idx * fox_w + cs.stop] = (
                ac * lax.rsqrt(ss * (1.0 / head_dim) + EPS) * gain_ref[:, cs]).astype(_BF16)
    qkv_ref[:, 2 * fox_w:3 * fox_w] = _dot_nt(u, wqkv_ref[2 * fox_w:3 * fox_w, :]).astype(_BF16)

    @pl.when(j == 0)
    def _():
        tail_s[...] = jnp.zeros(tail_s.shape, _F32)

    n_taps = cw_ref.shape[0]
    hr = tail_s.shape[0]
    assert n_taps - 1 <= hr and conv_w % (2 * LANES) == 0
    sub = lax.broadcasted_iota(jnp.int32, (hr, LANES), 0)
    for cp in range(conv_w // (2 * LANES)):
        gate_b = _dot_nt(u, wrest_ref[conv_w + cp * 2 * LANES:conv_w + (cp + 1) * 2 * LANES, :])
        for half in range(2):
            cs = slice((2 * cp + half) * LANES, (2 * cp + half + 1) * LANES)
            w_pair = jnp.concatenate(
                [wrest_ref[cs, :], wrest_ref[2 * conv_w + cs.start:2 * conv_w + cs.stop, :]], axis=0)
            r = _dot_nt(u, w_pair)
            zc = r[:, :LANES] * r[:, LANES:]
            y = cw_ref[n_taps - 1:n_taps, cs] * zc
            head = zc[0:hr, :]
            y_head = cw_ref[n_taps - 1:n_taps, cs] * head
            prev = tail_s[:, cs]
            for d in range(1, n_taps):
                w_d = cw_ref[n_taps - 1 - d:n_taps - d, cs]
                y = y + w_d * pltpu.roll(zc, d, axis=0)
                y_head = y_head + w_d * jnp.where(sub < d, pltpu.roll(prev, d, axis=0),
                                                  pltpu.roll(head, d, axis=0))
            gb = gate_b[:, half * LANES:(half + 1) * LANES]
            cc_ref[0:hr, cs] = (gb[0:hr, :] * y_head).astype(_BF16)
            cc_ref[hr:tm, cs] = (gb[hr:tm, :] * y[hr:tm, :]).astype(_BF16)
            tail_s[:, cs] = zc[tm - hr:tm, :]

    o = 3 * conv_w
    step = conv_w
    for i in range(gate_w // step):
        a = _dot_nt(u, wrest_ref[o + i * step:o + (i + 1) * step, :])
        gate_ref[:, i * step:(i + 1) * step] = jax.nn.sigmoid(a).astype(_BF16)

    a = _dot_nt(u, wf_ref[...]) + bf_ref[...]
    lf_ref[...] = jnp.minimum(a, 0.0) - jnp.log1p(jnp.exp(-jnp.abs(a)))


def _inproj(h, gain, w_qkv, w_rest, w_f, bf_pad, qg, kg, conv_w_l, *, batch, seq, dims):
    fox_w, conv_wd, gate_w, head_dim = dims
    n_tok, d_model = h.shape
    tm = TOKEN_TILE
    nt = seq // tm
    row = lambda b, j: (b * nt + j, 0)
    const2 = lambda b, j: (0, 0)
    return pl.pallas_call(
        functools.partial(_inproj_kernel, tm=tm, fox_w=fox_w, conv_w=conv_wd, gate_w=gate_w,
                          head_dim=head_dim),
        grid=(batch, nt),
        in_specs=[
            pl.BlockSpec((tm, d_model), row),
            _resident((1, d_model), const2),
            _resident(w_qkv.shape, const2),
            _resident(w_rest.shape, const2),
            _resident(w_f.shape, const2),
            _resident((1, LANES), const2),
            _resident((1, fox_w), const2),
            _resident((1, fox_w), const2),
            _resident(conv_w_l.shape, const2),
        ],
        out_specs=[
            pl.BlockSpec((tm, 3 * fox_w), row),
            pl.BlockSpec((tm, LANES), row),
            pl.BlockSpec((tm, conv_wd), row),
            pl.BlockSpec((tm, gate_w), row),
        ],
        out_shape=[
            jax.ShapeDtypeStruct((n_tok, 3 * fox_w), _BF16),
            jax.ShapeDtypeStruct((n_tok, LANES), _F32),
            jax.ShapeDtypeStruct((n_tok, conv_wd), _BF16),
            jax.ShapeDtypeStruct((n_tok, gate_w), _BF16),
        ],
        scratch_shapes=[pltpu.VMEM((2 * SUBLANES, conv_wd), _F32)],
        compiler_params=_params("arbitrary", "arbitrary"),
        name="inproj",
    )(h, gain, w_qkv, w_rest, w_f, bf_pad, qg, kg, conv_w_l)


def _decay_kernel(lf_ref, tri_ref, sq_ref, sk_ref, constq_ref, constk_ref, eq_ref, ek_ref,
                  *, seq, blk, n_heads):
    tri = tri_ref[...]
    lane = lax.broadcasted_iota(jnp.int32, (1, LANES), 1)
    carry = jnp.zeros((1, LANES), _F32)
    parts = []
    for i in range(seq // blk):
        hi, mid, lo = _split3(lf_ref[i * blk:(i + 1) * blk, :])
        t = _dot(tri, jnp.concatenate([lo, mid, hi], axis=1))
        c = (t[:, :LANES] + t[:, LANES:2 * LANES]) + t[:, 2 * LANES:] + carry
        carry = c[blk - 1:blk, :]
        hi, mid, lo = _split3(c * LOG2E)
        parts.append(jnp.where(lane < n_heads, hi, jnp.where(lane < 2 * n_heads, mid, lo)))
    parts = jnp.concatenate(parts, axis=0)
    eq_ref[...] = (_dot(parts, sq_ref[...]) + constq_ref[...]).astype(_BF16)
    ek_ref[...] = (_dot(parts, sk_ref[...]) + constk_ref[...]).astype(_BF16)


def _decay(logf, consts, *, batch, seq, n_heads):
    tri, sq = consts[0], consts[1]
    blk = tri.shape[0]
    aug_w = sq.shape[1]
    n_tok = logf.shape[0]
    const = lambda b: (0, 0)
    out = jax.ShapeDtypeStruct((n_tok, aug_w), _BF16)
    return pl.pallas_call(
        functools.partial(_decay_kernel, seq=seq, blk=blk, n_heads=n_heads),
        grid=(batch,),
        in_specs=[pl.BlockSpec((seq, LANES), lambda b: (b, 0))]
        + [_resident(c.shape, const) for c in consts],
        out_specs=[pl.BlockSpec((seq, aug_w), lambda b: (b, 0))] * 2,
        out_shape=[out, out],
        compiler_params=_params("arbitrary"),
        name="decay",
    )(logf, *consts)


def _decay_constants(n_heads, head_dim):
    assert DECAY_PARTS * n_heads <= LANES and 2 * DECAY_PARTS <= head_dim
    r = np.arange(CUMSUM_BLOCK)
    tri = (r[:, None] >= r[None, :]).astype(np.float32)
    width = (n_heads // 2) * LANES
    sq = np.zeros((LANES, width), np.float32)
    sk = np.zeros_like(sq)
    constq = np.zeros((1, width), np.float32)
    constk = np.zeros_like(constq)
    for h in range(n_heads):
        base = (h // 2) * LANES + (head_dim if h % 2 == 0 else 0)
        for part in range(DECAY_PARTS):
            sq[part * n_heads + h, base + part] = 1.0
            sk[part * n_heads + h, base + DECAY_PARTS + part] = -1.0
            constq[0, base + DECAY_PARTS + part] = 1.0
            constk[0, base + part] = 1.0
    return (jnp.asarray(tri, _BF16), jnp.asarray(sq, _BF16), jnp.asarray(sk, _BF16),
            jnp.asarray(constq), jnp.asarray(constk))


def _fox_kernel(q_ref, eq_ref, k_ref, ek_ref, v_ref, o_ref, *, seq, tq, tk, head_dim):
    lane = lax.broadcasted_iota(jnp.int32, (1, 2 * head_dim), 1)
    first = lane < head_dim
    pair_w = 2 * head_dim
    ones_col = jnp.where(lax.broadcasted_iota(jnp.int32, (tk, LANES), 1) == 0, 1.0, 0.0).astype(_BF16)

    def update(m, acc, q_rows, k_rows, v_rows, row_offset):
        s = _dot_nt(q_rows, k_rows)
        if row_offset is not None:
            rows = lax.broadcasted_iota(jnp.int32, s.shape, 0)
            cols = lax.broadcasted_iota(jnp.int32, s.shape, 1)
            s = jnp.where(rows + row_offset >= cols, s, MASK_VALUE)
        m_new = jnp.maximum(m, jnp.max(s, axis=1, keepdims=True))
        alpha = jnp.exp2(m - m_new)
        p = jnp.exp2(s - m_new).astype(_BF16)
        return m_new, alpha * acc + _dot(p, v_rows)

    n_q = seq // tq
    for pr in range(q_ref.shape[1] // pair_w):
        pc = slice(pr * pair_w, (pr + 1) * pair_w)
        q_aug, state = [], []
        for qi in range(n_q):
            qs = slice(qi * tq, (qi + 1) * tq)
            q = q_ref[qs, pc]
            eq = eq_ref[qs, pc]
            q_aug.append((jnp.where(first, q, eq), jnp.where(first, eq, q)))
            state.append([(jnp.full((tq, 1), MASK_VALUE, _F32),
                           jnp.zeros((tq, pair_w + LANES), _F32)) for _ in range(2)])

        for j in range(n_q):
            ks = slice(j * tk, (j + 1) * tk)
            kc = k_ref[ks, pc]
            v_aug = jnp.concatenate([v_ref[ks, pc], ones_col], axis=1)
            ekc = ek_ref[ks, pc]
            k_aug = (jnp.where(first, kc, ekc), jnp.where(first, ekc, kc))
            for qi in range(j, n_q):
                for e in range(2):
                    m, acc = state[qi][e]
                    state[qi][e] = update(m, acc, q_aug[qi][e], k_aug[e], v_aug,
                                          0 if j == qi else None)

        for qi in range(n_q):
            acc0, acc1 = state[qi][0][1], state[qi][1][1]
            o_ref[qi * tq:(qi + 1) * tq, pc] = jnp.where(
                first, acc0[:, :pair_w] / acc0[:, pair_w:pair_w + 1],
                acc1[:, :pair_w] / acc1[:, pair_w:pair_w + 1]).astype(_BF16)


def _fox(qkv, eq, ek, *, batch, seq, n_heads, head_dim):
    n_tok = qkv.shape[0]
    fox_w = n_heads * head_dim
    n_pairs = n_heads // 2
    pair_w = 2 * head_dim
    tq, tk = ATTN_Q_TILE, ATTN_K_TILE
    gw = ATTN_PAIRS_PER_STEP * pair_w
    n_groups = n_pairs // ATTN_PAIRS_PER_STEP
    assert tq == tk and pair_w == LANES and n_pairs % ATTN_PAIRS_PER_STEP == 0
    return pl.pallas_call(
        functools.partial(_fox_kernel, seq=seq, tq=tq, tk=tk, head_dim=head_dim),
        grid=(batch, n_groups),
        in_specs=[
            pl.BlockSpec((seq, gw), lambda b, p: (b, p)),
            pl.BlockSpec((seq, gw), lambda b, p: (b, p)),
            pl.BlockSpec((seq, gw), lambda b, p: (b, n_groups + p)),
            pl.BlockSpec((seq, gw), lambda b, p: (b, p)),
            pl.BlockSpec((seq, gw), lambda b, p: (b, 2 * n_groups + p)),
        ],
        out_specs=pl.BlockSpec((seq, gw), lambda b, p: (b, p)),
        out_shape=jax.ShapeDtypeStruct((n_tok, fox_w), _BF16),
        compiler_params=_params("arbitrary", "arbitrary"),
        name="fox",
    )(qkv, eq, qkv, ek, qkv)


def _post_kernel(a_ref, cc_ref, gate_ref, h_ref, wua_ref, wub_ref, wo_ref, gq_ref, wcq_ref,
                 cqg_ref, km_ref, vm_ref, wco_ref, *rest, n_heads, head_dim, n_staged):
    staged_src, (o_ref, *staged_dst, merged_s) = rest[:n_staged], rest[n_staged:]
    _cast_staged(staged_src, staged_dst)
    d_model = h_ref.shape[-1]
    scale = head_dim ** -0.5
    a = a_ref[...]
    cc = cc_ref[...]
    for c in range(d_model // MERGE_CHUNK):
        cs = slice(c * MERGE_CHUNK, (c + 1) * MERGE_CHUNK)
        up_a = _dot(a, wua_ref[:, cs])
        up_b = _dot(cc, wub_ref[:, cs])
        merged_s[:, cs] = (gate_ref[:, cs].astype(_F32) * up_a
                           + gate_ref[:, d_model + cs.start:d_model + cs.stop].astype(_F32) * up_b
                           ).astype(_BF16)
    h1 = h_ref[...] + _dot(merged_s[...], wo_ref[...])

    u = _rms_rows(h1, gq_ref[...]).astype(_BF16)
    qm = _dot(u, wcq_ref[...])
    heads = []
    for e in range(n_heads):
        sl = slice(e * head_dim, (e + 1) * head_dim)
        qe = qm[:, sl]
        ms = jnp.mean(qe * qe, axis=-1, keepdims=True)
        qn = (qe * lax.rsqrt(ms + EPS) * cqg_ref[...]).astype(_BF16)
        s = _dot_nt(qn, km_ref[:, sl]) * scale
        p = jnp.exp(s - jnp.max(s, axis=1, keepdims=True))
        den = jnp.sum(p, axis=1, keepdims=True)
        heads.append((_dot(p.astype(_BF16), vm_ref[:, sl]) / den).astype(_BF16))
    o_ref[...] = h1 + _dot(jnp.concatenate(heads, axis=1), wco_ref[...])


def _post(a_out, cc, gates, h, layer, w_up_a, w_up_b, w_o, gq, w_cq, cq_gain_l, kmem, vmem, w_co,
          staged, *, batch, seq):
    n_tok, d_model = h.shape
    tm = TOKEN_TILE
    nt = seq // tm
    fox_w = a_out.shape[1]
    conv_wd = cc.shape[1]
    mem_w = kmem.shape[-1]
    m_tok = kmem.shape[1] // batch
    head_dim = cq_gain_l.shape[-1]
    row = lambda b, j: (b * nt + j, 0)
    const2 = lambda b, j: (0, 0)
    st_in, st_out, st_shapes = _cast_staging(staged, batch * nt, lambda b, j: b * nt + j)
    return pl.pallas_call(
        functools.partial(_post_kernel, n_heads=mem_w // head_dim, head_dim=head_dim,
                          n_staged=len(staged)),
        grid=(batch, nt),
        in_specs=[
            pl.BlockSpec((tm, fox_w), row),
            pl.BlockSpec((tm, conv_wd), row),
            pl.BlockSpec((tm, 2 * d_model), row),
            pl.BlockSpec((tm, d_model), row),
            _resident(w_up_a.shape, const2),
            _resident(w_up_b.shape, const2),
            _resident(w_o.shape, const2),
            _resident((1, d_model), const2),
            _resident(w_cq.shape, const2),
            _resident((1, head_dim), const2),
            pl.BlockSpec((None, m_tok, mem_w), lambda b, j: (layer, b, 0)),
            pl.BlockSpec((None, m_tok, mem_w), lambda b, j: (layer, b, 0)),
            _resident(w_co.shape, const2),
        ] + st_in,
        out_specs=[pl.BlockSpec((tm, d_model), row)] + st_out,
        out_shape=[jax.ShapeDtypeStruct((n_tok, d_model), _F32)] + st_shapes,
        scratch_shapes=[pltpu.VMEM((tm, d_model), _BF16)],
        compiler_params=_params("arbitrary", "arbitrary"),
        name="post",
    )(a_out, cc, gates, h, w_up_a, w_up_b, w_o, gq, w_cq, cq_gain_l, kmem, vmem, w_co,
      *[s[0] for s in staged])


def _ffn_kernel(h_ref, g_ref, wgu_ref, wd_ref, *rest, d_ff, chunk, n_staged):
    staged_src, (o_ref, *staged_dst, act_s) = rest[:n_staged], rest[n_staged:]
    _cast_staged(staged_src, staged_dst)
    h = h_ref[...]
    u = _rms_rows(h, g_ref[...]).astype(_BF16)
    for c in range(d_ff // chunk):
        gate = _dot(u, wgu_ref[:, c * chunk:(c + 1) * chunk])
        up = _dot(u, wgu_ref[:, d_ff + c * chunk:d_ff + (c + 1) * chunk])
        act_s[:, c * chunk:(c + 1) * chunk] = (gate * jax.nn.sigmoid(gate) * up).astype(_BF16)
    o_ref[...] = h + _dot(act_s[...], wd_ref[...])


def _ffn(h, gain, w_gu, w_down, staged):
    n_tok, d_model = h.shape
    d_ff = w_down.shape[0]
    tm = TOKEN_TILE
    assert d_ff % FFN_CHUNK == 0
    st_in, st_out, st_shapes = _cast_staging(staged, n_tok // tm, lambda i: i)
    return pl.pallas_call(
        functools.partial(_ffn_kernel, d_ff=d_ff, chunk=FFN_CHUNK, n_staged=len(staged)),
        grid=(n_tok // tm,),
        in_specs=[
            pl.BlockSpec((tm, d_model), lambda i: (i, 0)),
            _resident((1, d_model), lambda i: (0, 0)),
            _resident(w_gu.shape, lambda i: (0, 0)),
            _resident(w_down.shape, lambda i: (0, 0)),
        ] + st_in,
        out_specs=[pl.BlockSpec((tm, d_model), lambda i: (i, 0))] + st_out,
        out_shape=[jax.ShapeDtypeStruct((n_tok, d_model), _F32)] + st_shapes,
        scratch_shapes=[pltpu.VMEM((tm, d_ff), _BF16)],
        compiler_params=_params("arbitrary"),
        name="ffn",
    )(h, gain, w_gu, w_down, *[s[0] for s in staged])


def kernel(x, mem, norm_mix, w_in, b_f, q_gain, k_gain, conv_w, w_up_a, w_up_b, w_o,
           norm_mem_q, norm_mem_kv, w_cq, w_ckv, cq_gain, ck_gain, w_co, norm_ffn, w_gu, w_down):
    batch, seq, d_model = x.shape
    n_layers = w_in.shape[0]
    head_dim = q_gain.shape[-1]
    n_heads = b_f.shape[-1]
    fox_w = n_heads * head_dim
    conv_wd = conv_w.shape[-1]
    gate_w = 2 * d_model
    assert seq % TOKEN_TILE == 0 and seq % ATTN_Q_TILE == 0 and seq % CUMSUM_BLOCK == 0
    assert w_in.shape[-1] == 3 * fox_w + n_heads + 3 * conv_wd + gate_w

    f0 = 3 * fox_w
    w_in_t = jnp.swapaxes(w_in, 1, 2)
    rest_rows = w_in_t.shape[1] - f0 - n_heads
    f_pad = LANES - DECAY_PARTS * n_heads
    w_f = jnp.pad(jnp.tile(w_in_t[:, f0:f0 + n_heads], (1, DECAY_PARTS, 1)),
                  ((0, 0), (0, f_pad), (0, 0))).astype(_BF16)
    bf_pad = jnp.pad(jnp.tile(b_f, (1, DECAY_PARTS)), ((0, 0), (0, f_pad)))

    def mixer_sources(layer):
        return [(w_in_t, layer, 0, f0), (w_in_t, layer, f0 + n_heads, rest_rows)] + [
            (w, layer, 0, w.shape[1]) for w in (w_up_a, w_up_b, w_o, w_cq, w_co)]

    qg = jnp.tile(q_gain, (1, n_heads)) * (head_dim ** -0.5 * LOG2E)
    kg = jnp.tile(k_gain, (1, n_heads))

    decay_consts = _decay_constants(n_heads, head_dim)

    kmem, vmem, *mixer_w = _memkv(mem, norm_mem_kv, w_ckv, ck_gain, mixer_sources(0))

    h = x.reshape(batch * seq, d_model)
    dims = (fox_w, conv_wd, gate_w, head_dim)
    for l in range(n_layers):
        w_qkv, w_rest, w_up_a_bf, w_up_b_bf, w_o_bf, w_cq_bf, w_co_bf = mixer_w
        qkv, logf, cc, gates = _inproj(h, norm_mix[l][None], w_qkv, w_rest, w_f[l], bf_pad[l][None],
                                       qg[l][None], kg[l][None], conv_w[l],
                                       batch=batch, seq=seq, dims=dims)
        eq, ek = _decay(logf, decay_consts, batch=batch, seq=seq, n_heads=n_heads)
        a_out = _fox(qkv, eq, ek, batch=batch, seq=seq, n_heads=n_heads, head_dim=head_dim)
        ffn_sources = [(w_gu, l, 0, w_gu.shape[1]), (w_down, l, 0, w_down.shape[1])]
        h, w_gu_bf, w_down_bf = _post(
            a_out, cc, gates, h, l, w_up_a_bf, w_up_b_bf, w_o_bf, norm_mem_q[l][None], w_cq_bf,
            cq_gain[l][None], kmem, vmem, w_co_bf, ffn_sources, batch=batch, seq=seq)
        next_sources = mixer_sources(l + 1) if l + 1 < n_layers else []
        h, *mixer_w = _ffn(h, norm_ffn[l][None], w_gu_bf, w_down_bf, next_sources)
    return h.reshape(batch, seq, d_model)
```

```python
import functools

import jax
import jax.numpy as jnp
import numpy as np
from jax import lax
from jax.experimental import pallas as pl
from jax.experimental.pallas import tpu as pltpu

_BF16 = jnp.bfloat16
_F32 = jnp.float32

EPS = 1e-6
LOG2E = 1.4426950408889634
LANES = 128
SUBLANES = 8
VMEM_LIMIT_BYTES = 56 * 1024 * 1024
MASK_VALUE = -1e30
TOKEN_TILE = 1024
MEMKV_SPLIT = 2
MERGE_CHUNK = 256
ATTN_Q_TILE = 512
ATTN_K_TILE = 512
ATTN_PAIRS_PER_STEP = 2
CUMSUM_BLOCK = 256
DECAY_PARTS = 3
FFN_CHUNK = 256


def _dot(a, b):
    return jnp.dot(a, b, preferred_element_type=_F32)


def _dot_nt(a, b):
    return lax.dot_general(a, b, (((1,), (1,)), ((), ())), preferred_element_type=_F32)


def _rms_rows(x, gain):
    ms = jnp.mean(x * x, axis=-1, keepdims=True)
    return x * lax.rsqrt(ms + EPS) * gain


def _split3(x):
    hi = x.astype(_BF16)
    r1 = x - hi.astype(_F32)
    mid = r1.astype(_BF16)
    lo = (r1 - mid.astype(_F32)).astype(_BF16)
    return hi, mid, lo


def _params(*sem):
    return pltpu.CompilerParams(dimension_semantics=sem, vmem_limit_bytes=VMEM_LIMIT_BYTES)


def _resident(block_shape, index_map):
    return pl.BlockSpec(block_shape, index_map, pipeline_mode=pl.Buffered(1))


def _cast_staging(sources, n_steps, step_of):
    in_specs, out_specs, out_shapes = [], [], []
    for arr, layer, first_row, n_rows in sources:
        cols = arr.shape[-1]
        slab = n_rows // n_steps
        assert slab * n_steps == n_rows and slab % (2 * SUBLANES) == 0 and first_row % SUBLANES == 0
        in_specs.append(pl.BlockSpec(
            (None, pl.Element(slab), pl.Element(cols)),
            lambda *g, layer=layer, first_row=first_row, slab=slab:
                (layer, pl.multiple_of(first_row + slab * step_of(*g), SUBLANES), 0)))
        out_specs.append(pl.BlockSpec((slab, cols), lambda *g: (step_of(*g), 0)))
        out_shapes.append(jax.ShapeDtypeStruct((n_rows, cols), _BF16))
    return in_specs, out_specs, out_shapes


def _cast_staged(src_refs, dst_refs):
    for src, dst in zip(src_refs, dst_refs, strict=True):
        dst[...] = src[...].astype(_BF16)


def _memkv_kernel(mem_ref, g_ref, w_ref, ckg_ref, *rest, n_heads, head_dim, n_staged):
    staged_src, (k_ref, v_ref, *staged_dst) = rest[:n_staged], rest[n_staged:]
    _cast_staged(staged_src, staged_dst)
    u = _rms_rows(mem_ref[...], g_ref[...]).astype(_BF16)
    kv = _dot(u, w_ref[...].astype(_BF16))
    width = n_heads * head_dim
    for e in range(n_heads):
        ke = kv[:, e * head_dim:(e + 1) * head_dim]
        ms = jnp.mean(ke * ke, axis=-1, keepdims=True)
        k_ref[:, e * head_dim:(e + 1) * head_dim] = (
            ke * lax.rsqrt(ms + EPS) * ckg_ref[...]).astype(_BF16)
    v_ref[...] = kv[:, width:].astype(_BF16)


def _memkv(mem, norm_mem_kv, w_ckv, ck_gain, staged):
    n_layers, d_model, two_w = w_ckv.shape
    batch, m_tok, _ = mem.shape
    rows = batch * m_tok
    blk = rows // MEMKV_SPLIT
    width = two_w // 2
    head_dim = ck_gain.shape[-1]
    n_heads = width // head_dim
    out = jax.ShapeDtypeStruct((n_layers, rows, width), _BF16)
    st_in, st_out, st_shapes = _cast_staging(staged, n_layers * MEMKV_SPLIT,
                                             lambda l, s: l * MEMKV_SPLIT + s)
    return pl.pallas_call(
        functools.partial(_memkv_kernel, n_heads=n_heads, head_dim=head_dim,
                          n_staged=len(staged)),
        grid=(n_layers, MEMKV_SPLIT),
        in_specs=[
            pl.BlockSpec((blk, d_model), lambda l, s: (s, 0)),
            pl.BlockSpec((None, 1, d_model), lambda l, s: (l, 0, 0)),
            pl.BlockSpec((None, d_model, two_w), lambda l, s: (l, 0, 0)),
            pl.BlockSpec((None, 1, head_dim), lambda l, s: (l, 0, 0)),
        ] + st_in,
        out_specs=[
            pl.BlockSpec((None, blk, width), lambda l, s: (l, s, 0)),
            pl.BlockSpec((None, blk, width), lambda l, s: (l, s, 0)),
        ] + st_out,
        out_shape=[out, out] + st_shapes,
        compiler_params=_params("arbitrary", "arbitrary"),
        name="memkv",
    )(mem.reshape(rows, d_model), norm_mem_kv[:, None, :], w_ckv, ck_gain[:, None, :],
      *[s[0] for s in staged])


def _inproj_kernel(hfirst_ref, hnext_ref, g_ref, wqkv_ref, wrest_ref, wf_ref, bf_ref, qg_ref,
                   kg_ref, cw_ref, qkv_ref, lf_ref, cc_ref, gate_ref, tail_s, u_a, u_b, **dims):
    step = pl.program_id(0) * pl.num_programs(1) + pl.program_id(1)

    @pl.when(step == 0)
    def _():
        u_a[...] = _rms_rows(hfirst_ref[...], g_ref[...]).astype(_BF16)

    @pl.when(pl.program_id(1) == 0)
    def _():
        tail_s[...] = jnp.zeros(tail_s.shape, _F32)

    def run(u_cur, u_nxt):
        _inproj_tile(u_cur[...], wqkv_ref, wrest_ref, wf_ref, bf_ref, qg_ref, kg_ref, cw_ref,
                     qkv_ref, lf_ref, cc_ref, gate_ref, tail_s, **dims)
        u_nxt[...] = _rms_rows(hnext_ref[...], g_ref[...]).astype(_BF16)

    @pl.when(step % 2 == 0)
    def _():
        run(u_a, u_b)

    @pl.when(step % 2 == 1)
    def _():
        run(u_b, u_a)


def _inproj_tile(u, wqkv_ref, wrest_ref, wf_ref, bf_ref, qg_ref, kg_ref, cw_ref,
                 qkv_ref, lf_ref, cc_ref, gate_ref, tail_s,
                 *, tm, fox_w, conv_w, gate_w, head_dim):
    assert 2 * head_dim == LANES
    first = lax.broadcasted_iota(jnp.int32, (1, LANES), 1) < head_dim
    for idx, gain_ref in ((0, qg_ref), (1, kg_ref)):
        a = _dot_nt(u, wqkv_ref[idx * fox_w:(idx + 1) * fox_w, :])
        for c in range(fox_w // LANES):
            cs = slice(c * LANES, (c + 1) * LANES)
            ac = a[:, cs]
            sq = ac * ac
            lo = jnp.sum(jnp.where(first, sq, 0.0), axis=1, keepdims=True)
            hi = jnp.sum(jnp.where(first, 0.0, sq), axis=1, keepdims=True)
            ss = jnp.where(first, lo, hi)
            qkv_ref[:, idx * fox_w + cs.start:idx * fox_w + cs.stop] = (
                ac * lax.rsqrt(ss * (1.0 / head_dim) + EPS) * gain_ref[:, cs]).astype(_BF16)
    qkv_ref[:, 2 * fox_w:3 * fox_w] = _dot_nt(u, wqkv_ref[2 * fox_w:3 * fox_w, :]).astype(_BF16)

    n_taps = cw_ref.shape[0]
    hr = tail_s.shape[0]
    assert n_taps - 1 <= hr and conv_w % (2 * LANES) == 0
    sub = lax.broadcasted_iota(jnp.int32, (hr, LANES), 0)
    for cp in range(conv_w // (2 * LANES)):
        gate_b = _dot_nt(u, wrest_ref[conv_w + cp * 2 * LANES:conv_w + (cp + 1) * 2 * LANES, :])
        for half in range(2):
            cs = slice((2 * cp + half) * LANES, (2 * cp + half + 1) * LANES)
            w_pair = jnp.concatenate(
                [wrest_ref[cs, :], wrest_ref[2 * conv_w + cs.start:2 * conv_w + cs.stop, :]], axis=0)
            r = _dot_nt(u, w_pair)
            zc = r[:, :LANES] * r[:, LANES:]
            y = cw_ref[n_taps - 1:n_taps, cs] * zc
            head = zc[0:hr, :]
            y_head = cw_ref[n_taps - 1:n_taps, cs] * head
            prev = tail_s[:, cs]
            for d in range(1, n_taps):
                w_d = cw_ref[n_taps - 1 - d:n_taps - d, cs]
                y = y + w_d * pltpu.roll(zc, d, axis=0)
                y_head = y_head + w_d * jnp.where(sub < d, pltpu.roll(prev, d, axis=0),
                                                  pltpu.roll(head, d, axis=0))
            gb = gate_b[:, half * LANES:(half + 1) * LANES]
            cc_ref[0:hr, cs] = (gb[0:hr, :] * y_head).astype(_BF16)
            cc_ref[hr:tm, cs] = (gb[hr:tm, :] * y[hr:tm, :]).astype(_BF16)
            tail_s[:, cs] = zc[tm - hr:tm, :]

    o = 3 * conv_w
    step = conv_w
    for i in range(gate_w // step):
        a = _dot_nt(u, wrest_ref[o + i * step:o + (i + 1) * step, :])
        gate_ref[:, i * step:(i + 1) * step] = jax.nn.sigmoid(a).astype(_BF16)

    a = _dot_nt(u, wf_ref[...]) + bf_ref[...]
    lf_ref[...] = jnp.minimum(a, 0.0) - jnp.log1p(jnp.exp(-jnp.abs(a)))


def _inproj(h, gain, w_qkv, w_rest, w_f, bf_pad, qg, kg, conv_w_l, *, batch, seq, dims):
    fox_w, conv_wd, gate_w, head_dim = dims
    n_tok, d_model = h.shape
    tm = TOKEN_TILE
    nt = seq // tm
    row = lambda b, j: (b * nt + j, 0)
    const2 = lambda b, j: (0, 0)
    return pl.pallas_call(
        functools.partial(_inproj_kernel, tm=tm, fox_w=fox_w, conv_w=conv_wd, gate_w=gate_w,
                          head_dim=head_dim),
        grid=(batch, nt),
        in_specs=[
            _resident((tm, d_model), const2),
            pl.BlockSpec((tm, d_model),
                         lambda b, j: (jnp.minimum(b * nt + j + 1, batch * nt - 1), 0)),
            _resident((1, d_model), const2),
            _resident(w_qkv.shape, const2),
            _resident(w_rest.shape, const2),
            _resident(w_f.shape, const2),
            _resident((1, LANES), const2),
            _resident((1, fox_w), const2),
            _resident((1, fox_w), const2),
            _resident(conv_w_l.shape, const2),
        ],
        out_specs=[
            pl.BlockSpec((tm, 3 * fox_w), row),
            pl.BlockSpec((tm, LANES), row),
            pl.BlockSpec((tm, conv_wd), row),
            pl.BlockSpec((tm, gate_w), row),
        ],
        out_shape=[
            jax.ShapeDtypeStruct((n_tok, 3 * fox_w), _BF16),
            jax.ShapeDtypeStruct((n_tok, LANES), _F32),
            jax.ShapeDtypeStruct((n_tok, conv_wd), _BF16),
            jax.ShapeDtypeStruct((n_tok, gate_w), _BF16),
        ],
        scratch_shapes=[pltpu.VMEM((2 * SUBLANES, conv_wd), _F32),
                        pltpu.VMEM((tm, d_model), _BF16), pltpu.VMEM((tm, d_model), _BF16)],
        compiler_params=_params("arbitrary", "arbitrary"),
        name="inproj",
    )(h, h, gain, w_qkv, w_rest, w_f, bf_pad, qg, kg, conv_w_l)


def _decay_kernel(lf_ref, tri_ref, sq_ref, sk_ref, constq_ref, constk_ref, eq_ref, ek_ref,
                  *, seq, blk, n_heads):
    tri = tri_ref[...]
    lane = lax.broadcasted_iota(jnp.int32, (1, LANES), 1)
    carry = jnp.zeros((1, LANES), _F32)
    parts = []
    for i in range(seq // blk):
        hi, mid, lo = _split3(lf_ref[i * blk:(i + 1) * blk, :])
        t = _dot(tri, jnp.concatenate([lo, mid, hi], axis=1))
        c = (t[:, :LANES] + t[:, LANES:2 * LANES]) + t[:, 2 * LANES:] + carry
        carry = c[blk - 1:blk, :]
        hi, mid, lo = _split3(c * LOG2E)
        parts.append(jnp.where(lane < n_heads, hi, jnp.where(lane < 2 * n_heads, mid, lo)))
    parts = jnp.concatenate(parts, axis=0)
    eq_ref[...] = (_dot(parts, sq_ref[...]) + constq_ref[...]).astype(_BF16)
    ek_ref[...] = (_dot(parts, sk_ref[...]) + constk_ref[...]).astype(_BF16)


def _decay(logf, consts, *, batch, seq, n_heads):
    tri, sq = consts[0], consts[1]
    blk = tri.shape[0]
    aug_w = sq.shape[1]
    n_tok = logf.shape[0]
    const = lambda b: (0, 0)
    out = jax.ShapeDtypeStruct((n_tok, aug_w), _BF16)
    return pl.pallas_call(
        functools.partial(_decay_kernel, seq=seq, blk=blk, n_heads=n_heads),
        grid=(batch,),
        in_specs=[pl.BlockSpec((seq, LANES), lambda b: (b, 0))]
        + [_resident(c.shape, const) for c in consts],
        out_specs=[pl.BlockSpec((seq, aug_w), lambda b: (b, 0))] * 2,
        out_shape=[out, out],
        compiler_params=_params("arbitrary"),
        name="decay",
    )(logf, *consts)


def _decay_constants(n_heads, head_dim):
    assert DECAY_PARTS * n_heads <= LANES and 2 * DECAY_PARTS <= head_dim
    r = np.arange(CUMSUM_BLOCK)
    tri = (r[:, None] >= r[None, :]).astype(np.float32)
    width = (n_heads // 2) * LANES
    sq = np.zeros((LANES, width), np.float32)
    sk = np.zeros_like(sq)
    constq = np.zeros((1, width), np.float32)
    constk = np.zeros_like(constq)
    for h in range(n_heads):
        base = (h // 2) * LANES + (head_dim if h % 2 == 0 else 0)
        for part in range(DECAY_PARTS):
            sq[part * n_heads + h, base + part] = 1.0
            sk[part * n_heads + h, base + DECAY_PARTS + part] = -1.0
            constq[0, base + DECAY_PARTS + part] = 1.0
            constk[0, base + part] = 1.0
    return (jnp.asarray(tri, _BF16), jnp.asarray(sq, _BF16), jnp.asarray(sk, _BF16),
            jnp.asarray(constq), jnp.asarray(constk))


def _fox_kernel(q_ref, eq_ref, k_ref, ek_ref, v_ref, o_ref, *, seq, tq, tk, head_dim):
    lane = lax.broadcasted_iota(jnp.int32, (1, 2 * head_dim), 1)
    first = lane < head_dim
    pair_w = 2 * head_dim
    ones_col = jnp.where(lax.broadcasted_iota(jnp.int32, (tk, LANES), 1) == 0, 1.0, 0.0).astype(_BF16)

    def update(m, acc, q_rows, k_rows, v_rows, row_offset):
        s = _dot_nt(q_rows, k_rows)
        if row_offset is not None:
            rows = lax.broadcasted_iota(jnp.int32, s.shape, 0)
            cols = lax.broadcasted_iota(jnp.int32, s.shape, 1)
            s = jnp.where(rows + row_offset >= cols, s, MASK_VALUE)
        m_new = jnp.maximum(m, jnp.max(s, axis=1, keepdims=True))
        alpha = jnp.exp2(m - m_new)
        p = jnp.exp2(s - m_new).astype(_BF16)
        return m_new, alpha * acc + _dot(p, v_rows)

    n_q = seq // tq
    for pr in range(q_ref.shape[1] // pair_w):
        pc = slice(pr * pair_w, (pr + 1) * pair_w)
        q_aug, state = [], []
        for qi in range(n_q):
            qs = slice(qi * tq, (qi + 1) * tq)
            q = q_ref[qs, pc]
            eq = eq_ref[qs, pc]
            q_aug.append((jnp.where(first, q, eq), jnp.where(first, eq, q)))
            state.append([(jnp.full((tq, 1), MASK_VALUE, _F32),
                           jnp.zeros((tq, pair_w + LANES), _F32)) for _ in range(2)])

        for j in range(n_q):
            ks = slice(j * tk, (j + 1) * tk)
            kc = k_ref[ks, pc]
            v_aug = jnp.concatenate([v_ref[ks, pc], ones_col], axis=1)
            ekc = ek_ref[ks, pc]
            k_aug = (jnp.where(first, kc, ekc), jnp.where(first, ekc, kc))
            for qi in range(j, n_q):
                for e in range(2):
                    m, acc = state[qi][e]
                    state[qi][e] = update(m, acc, q_aug[qi][e], k_aug[e], v_aug,
                                          0 if j == qi else None)

        for qi in range(n_q):
            acc0, acc1 = state[qi][0][1], state[qi][1][1]
            o_ref[qi * tq:(qi + 1) * tq, pc] = jnp.where(
                first, acc0[:, :pair_w] / acc0[:, pair_w:pair_w + 1],
                acc1[:, :pair_w] / acc1[:, pair_w:pair_w + 1]).astype(_BF16)


def _fox(qkv, eq, ek, *, batch, seq, n_heads, head_dim):
    n_tok = qkv.shape[0]
    fox_w = n_heads * head_dim
    n_pairs = n_heads // 2
    pair_w = 2 * head_dim
    tq, tk = ATTN_Q_TILE, ATTN_K_TILE
    gw = ATTN_PAIRS_PER_STEP * pair_w
    n_groups = n_pairs // ATTN_PAIRS_PER_STEP
    assert tq == tk and pair_w == LANES and n_pairs % ATTN_PAIRS_PER_STEP == 0
    return pl.pallas_call(
        functools.partial(_fox_kernel, seq=seq, tq=tq, tk=tk, head_dim=head_dim),
        grid=(batch, n_groups),
        in_specs=[
            pl.BlockSpec((seq, gw), lambda b, p: (b, p)),
            pl.BlockSpec((seq, gw), lambda b, p: (b, p)),
            pl.BlockSpec((seq, gw), lambda b, p: (b, n_groups + p)),
            pl.BlockSpec((seq, gw), lambda b, p: (b, p)),
            pl.BlockSpec((seq, gw), lambda b, p: (b, 2 * n_groups + p)),
        ],
        out_specs=pl.BlockSpec((seq, gw), lambda b, p: (b, p)),
        out_shape=jax.ShapeDtypeStruct((n_tok, fox_w), _BF16),
        compiler_params=_params("arbitrary", "arbitrary"),
        name="fox",
    )(qkv, eq, qkv, ek, qkv)


def _post_kernel(a_ref, cc_ref, gate_ref, h_ref, wua_ref, wub_ref, wo_ref, gq_ref, wcq_ref,
                 cqg_ref, km_ref, vm_ref, wco_ref, *rest, n_heads, head_dim, n_staged):
    staged_src, (o_ref, *staged_dst, merged_s) = rest[:n_staged], rest[n_staged:]
    _cast_staged(staged_src, staged_dst)
    d_model = h_ref.shape[-1]
    scale = head_dim ** -0.5
    a = a_ref[...]
    cc = cc_ref[...]
    for c in range(d_model // MERGE_CHUNK):
        cs = slice(c * MERGE_CHUNK, (c + 1) * MERGE_CHUNK)
        up_a = _dot(a, wua_ref[:, cs])
        up_b = _dot(cc, wub_ref[:, cs])
        merged_s[:, cs] = (gate_ref[:, cs].astype(_F32) * up_a
                           + gate_ref[:, d_model + cs.start:d_model + cs.stop].astype(_F32) * up_b
                           ).astype(_BF16)
    h1 = h_ref[...] + _dot(merged_s[...], wo_ref[...])

    u = _rms_rows(h1, gq_ref[...]).astype(_BF16)
    qm = _dot(u, wcq_ref[...])
    heads = []
    for e in range(n_heads):
        sl = slice(e * head_dim, (e + 1) * head_dim)
        qe = qm[:, sl]
        ms = jnp.mean(qe * qe, axis=-1, keepdims=True)
        qn = (qe * lax.rsqrt(ms + EPS) * cqg_ref[...]).astype(_BF16)
        s = _dot_nt(qn, km_ref[:, sl]) * scale
        p = jnp.exp(s - jnp.max(s, axis=1, keepdims=True))
        den = jnp.sum(p, axis=1, keepdims=True)
        heads.append((_dot(p.astype(_BF16), vm_ref[:, sl]) / den).astype(_BF16))
    o_ref[...] = h1 + _dot(jnp.concatenate(heads, axis=1), wco_ref[...])


def _post(a_out, cc, gates, h, layer, w_up_a, w_up_b, w_o, gq, w_cq, cq_gain_l, kmem, vmem, w_co,
          staged, *, batch, seq):
    n_tok, d_model = h.shape
    tm = TOKEN_TILE
    nt = seq // tm
    fox_w = a_out.shape[1]
    conv_wd = cc.shape[1]
    mem_w = kmem.shape[-1]
    m_tok = kmem.shape[1] // batch
    head_dim = cq_gain_l.shape[-1]
    row = lambda b, j: (b * nt + j, 0)
    const2 = lambda b, j: (0, 0)
    st_in, st_out, st_shapes = _cast_staging(staged, batch * nt, lambda b, j: b * nt + j)
    return pl.pallas_call(
        functools.partial(_post_kernel, n_heads=mem_w // head_dim, head_dim=head_dim,
                          n_staged=len(staged)),
        grid=(batch, nt),
        in_specs=[
            pl.BlockSpec((tm, fox_w), row),
            pl.BlockSpec((tm, conv_wd), row),
            pl.BlockSpec((tm, 2 * d_model), row),
            pl.BlockSpec((tm, d_model), row),
            _resident(w_up_a.shape, const2),
            _resident(w_up_b.shape, const2),
            _resident(w_o.shape, const2),
            _resident((1, d_model), const2),
            _resident(w_cq.shape, const2),
            _resident((1, head_dim), const2),
            pl.BlockSpec((None, m_tok, mem_w), lambda b, j: (layer, b, 0)),
            pl.BlockSpec((None, m_tok, mem_w), lambda b, j: (layer, b, 0)),
            _resident(w_co.shape, const2),
        ] + st_in,
        out_specs=[pl.BlockSpec((tm, d_model), row)] + st_out,
        out_shape=[jax.ShapeDtypeStruct((n_tok, d_model), _F32)] + st_shapes,
        scratch_shapes=[pltpu.VMEM((tm, d_model), _BF16)],
        compiler_params=_params("arbitrary", "arbitrary"),
        name="post",
    )(a_out, cc, gates, h, w_up_a, w_up_b, w_o, gq, w_cq, cq_gain_l, kmem, vmem, w_co,
      *[s[0] for s in staged])


def _ffn_kernel(h_ref, g_ref, wgu_ref, wd_ref, *rest, d_ff, chunk, n_staged):
    staged_src, (o_ref, *staged_dst, act_s) = rest[:n_staged], rest[n_staged:]
    _cast_staged(staged_src, staged_dst)
    h = h_ref[...]
    u = _rms_rows(h, g_ref[...]).astype(_BF16)
    for c in range(d_ff // chunk):
        gate = _dot(u, wgu_ref[:, c * chunk:(c + 1) * chunk])
        up = _dot(u, wgu_ref[:, d_ff + c * chunk:d_ff + (c + 1) * chunk])
        act_s[:, c * chunk:(c + 1) * chunk] = (gate * jax.nn.sigmoid(gate) * up).astype(_BF16)
    o_ref[...] = h + _dot(act_s[...], wd_ref[...])


def _ffn(h, gain, w_gu, w_down, staged):
    n_tok, d_model = h.shape
    d_ff = w_down.shape[0]
    tm = TOKEN_TILE
    assert d_ff % FFN_CHUNK == 0
    st_in, st_out, st_shapes = _cast_staging(staged, n_tok // tm, lambda i: i)
    return pl.pallas_call(
        functools.partial(_ffn_kernel, d_ff=d_ff, chunk=FFN_CHUNK, n_staged=len(staged)),
        grid=(n_tok // tm,),
        in_specs=[
            pl.BlockSpec((tm, d_model), lambda i: (i, 0)),
            _resident((1, d_model), lambda i: (0, 0)),
            _resident(w_gu.shape, lambda i: (0, 0)),
            _resident(w_down.shape, lambda i: (0, 0)),
        ] + st_in,
        out_specs=[pl.BlockSpec((tm, d_model), lambda i: (i, 0))] + st_out,
        out_shape=[jax.ShapeDtypeStruct((n_tok, d_model), _F32)] + st_shapes,
        scratch_shapes=[pltpu.VMEM((tm, d_ff), _BF16)],
        compiler_params=_params("arbitrary"),
        name="ffn",
    )(h, gain, w_gu, w_down, *[s[0] for s in staged])


def kernel(x, mem, norm_mix, w_in, b_f, q_gain, k_gain, conv_w, w_up_a, w_up_b, w_o,
           norm_mem_q, norm_mem_kv, w_cq, w_ckv, cq_gain, ck_gain, w_co, norm_ffn, w_gu, w_down):
    batch, seq, d_model = x.shape
    n_layers = w_in.shape[0]
    head_dim = q_gain.shape[-1]
    n_heads = b_f.shape[-1]
    fox_w = n_heads * head_dim
    conv_wd = conv_w.shape[-1]
    gate_w = 2 * d_model
    assert seq % TOKEN_TILE == 0 and seq % ATTN_Q_TILE == 0 and seq % CUMSUM_BLOCK == 0
    assert w_in.shape[-1] == 3 * fox_w + n_heads + 3 * conv_wd + gate_w

    f0 = 3 * fox_w
    w_in_t = jnp.swapaxes(w_in, 1, 2)
    rest_rows = w_in_t.shape[1] - f0 - n_heads
    f_pad = LANES - DECAY_PARTS * n_heads
    w_f = jnp.pad(jnp.tile(w_in_t[:, f0:f0 + n_heads], (1, DECAY_PARTS, 1)),
                  ((0, 0), (0, f_pad), (0, 0))).astype(_BF16)
    bf_pad = jnp.pad(jnp.tile(b_f, (1, DECAY_PARTS)), ((0, 0), (0, f_pad)))

    def mixer_sources(layer):
        return [(w_in_t, layer, 0, f0), (w_in_t, layer, f0 + n_heads, rest_rows)] + [
            (w, layer, 0, w.shape[1]) for w in (w_up_a, w_up_b, w_o, w_cq, w_co)]

    qg = jnp.tile(q_gain, (1, n_heads)) * (head_dim ** -0.5 * LOG2E)
    kg = jnp.tile(k_gain, (1, n_heads))

    decay_consts = _decay_constants(n_heads, head_dim)

    kmem, vmem, *mixer_w = _memkv(mem, norm_mem_kv, w_ckv, ck_gain, mixer_sources(0))

    h = x.reshape(batch * seq, d_model)
    dims = (fox_w, conv_wd, gate_w, head_dim)
    for l in range(n_layers):
        w_qkv, w_rest, w_up_a_bf, w_up_b_bf, w_o_bf, w_cq_bf, w_co_bf = mixer_w
        qkv, logf, cc, gates = _inproj(h, norm_mix[l][None], w_qkv, w_rest, w_f[l], bf_pad[l][None],
                                       qg[l][None], kg[l][None], conv_w[l],
                                       batch=batch, seq=seq, dims=dims)
        eq, ek = _decay(logf, decay_consts, batch=batch, seq=seq, n_heads=n_heads)
        a_out = _fox(qkv, eq, ek, batch=batch, seq=seq, n_heads=n_heads, head_dim=head_dim)
        ffn_sources = [(w_gu, l, 0, w_gu.shape[1]), (w_down, l, 0, w_down.shape[1])]
        h, w_gu_bf, w_down_bf = _post(
            a_out, cc, gates, h, l, w_up_a_bf, w_up_b_bf, w_o_bf, norm_mem_q[l][None], w_cq_bf,
            cq_gain[l][None], kmem, vmem, w_co_bf, ffn_sources, batch=batch, seq=seq)
        next_sources = mixer_sources(l + 1) if l + 1 < n_layers else []
        h, *mixer_w = _ffn(h, norm_ffn[l][None], w_gu_bf, w_down_bf, next_sources)
    return h.reshape(batch, seq, d_model)
```

```python
import functools

import jax
import jax.numpy as jnp
import numpy as np
from jax import lax
from jax.experimental import pallas as pl
from jax.experimental.pallas import tpu as pltpu

_BF16 = jnp.bfloat16
_F32 = jnp.float32

EPS = 1e-6
LOG2E = 1.4426950408889634
LANES = 128
SUBLANES = 8
VMEM_LIMIT_BYTES = 56 * 1024 * 1024
MASK_VALUE = -1e30
TOKEN_TILE = 1024
MEMKV_SPLIT = 2
MERGE_CHUNK = 256
ATTN_Q_TILE = 512
ATTN_K_TILE = 512
ATTN_PAIRS_PER_STEP = 2
CUMSUM_BLOCK = 256
DECAY_SEQS_PER_STEP = 2
DECAY_PARTS = 3
FFN_CHUNK = 256


def _dot(a, b):
    return jnp.dot(a, b, preferred_element_type=_F32)


def _dot_nt(a, b):
    return lax.dot_general(a, b, (((1,), (1,)), ((), ())), preferred_element_type=_F32)


def _rms_rows(x, gain):
    ms = jnp.mean(x * x, axis=-1, keepdims=True)
    return x * lax.rsqrt(ms + EPS) * gain


def _split3(x):
    hi = x.astype(_BF16)
    r1 = x - hi.astype(_F32)
    mid = r1.astype(_BF16)
    lo = (r1 - mid.astype(_F32)).astype(_BF16)
    return hi, mid, lo


def _params(*sem):
    return pltpu.CompilerParams(dimension_semantics=sem, vmem_limit_bytes=VMEM_LIMIT_BYTES)


def _resident(block_shape, index_map):
    return pl.BlockSpec(block_shape, index_map, pipeline_mode=pl.Buffered(1))


def _cast_staging(sources, n_steps, step_of):
    in_specs, out_specs, out_shapes = [], [], []
    for arr, layer, first_row, n_rows in sources:
        cols = arr.shape[-1]
        slab = n_rows // n_steps
        assert slab * n_steps == n_rows and slab % (2 * SUBLANES) == 0 and first_row % SUBLANES == 0
        in_specs.append(pl.BlockSpec(
            (None, pl.Element(slab), pl.Element(cols)),
            lambda *g, layer=layer, first_row=first_row, slab=slab:
                (layer, pl.multiple_of(first_row + slab * step_of(*g), SUBLANES), 0)))
        out_specs.append(pl.BlockSpec((slab, cols), lambda *g: (step_of(*g), 0)))
        out_shapes.append(jax.ShapeDtypeStruct((n_rows, cols), _BF16))
    return in_specs, out_specs, out_shapes


def _cast_staged(src_refs, dst_refs):
    for src, dst in zip(src_refs, dst_refs, strict=True):
        dst[...] = src[...].astype(_BF16)


def _memkv_kernel(mem_ref, g_ref, w_ref, ckg_ref, *rest, n_heads, head_dim, n_staged):
    staged_src, (k_ref, v_ref, *staged_dst) = rest[:n_staged], rest[n_staged:]
    _cast_staged(staged_src, staged_dst)
    u = _rms_rows(mem_ref[...], g_ref[...]).astype(_BF16)
    kv = _dot(u, w_ref[...].astype(_BF16))
    width = n_heads * head_dim
    for e in range(n_heads):
        ke = kv[:, e * head_dim:(e + 1) * head_dim]
        ms = jnp.mean(ke * ke, axis=-1, keepdims=True)
        k_ref[:, e * head_dim:(e + 1) * head_dim] = (
            ke * lax.rsqrt(ms + EPS) * ckg_ref[...]).astype(_BF16)
    v_ref[...] = kv[:, width:].astype(_BF16)


def _memkv(mem, norm_mem_kv, w_ckv, ck_gain, staged):
    n_layers, d_model, two_w = w_ckv.shape
    batch, m_tok, _ = mem.shape
    rows = batch * m_tok
    blk = rows // MEMKV_SPLIT
    width = two_w // 2
    head_dim = ck_gain.shape[-1]
    n_heads = width // head_dim
    out = jax.ShapeDtypeStruct((n_layers, rows, width), _BF16)
    st_in, st_out, st_shapes = _cast_staging(staged, n_layers * MEMKV_SPLIT,
                                             lambda l, s: l * MEMKV_SPLIT + s)
    return pl.pallas_call(
        functools.partial(_memkv_kernel, n_heads=n_heads, head_dim=head_dim,
                          n_staged=len(staged)),
        grid=(n_layers, MEMKV_SPLIT),
        in_specs=[
            pl.BlockSpec((blk, d_model), lambda l, s: (s, 0)),
            pl.BlockSpec((None, 1, d_model), lambda l, s: (l, 0, 0)),
            pl.BlockSpec((None, d_model, two_w), lambda l, s: (l, 0, 0)),
            pl.BlockSpec((None, 1, head_dim), lambda l, s: (l, 0, 0)),
        ] + st_in,
        out_specs=[
            pl.BlockSpec((None, blk, width), lambda l, s: (l, s, 0)),
            pl.BlockSpec((None, blk, width), lambda l, s: (l, s, 0)),
        ] + st_out,
        out_shape=[out, out] + st_shapes,
        compiler_params=_params("arbitrary", "arbitrary"),
        name="memkv",
    )(mem.reshape(rows, d_model), norm_mem_kv[:, None, :], w_ckv, ck_gain[:, None, :],
      *[s[0] for s in staged])


def _inproj_kernel(h_ref, g_ref, wqkv_ref, wrest_ref, wf_ref, bf_ref, qg_ref, kg_ref, cw_ref,
                   qkv_ref, lf_ref, cc_ref, gate_ref, tail_s,
                   *, tm, fox_w, conv_w, gate_w, head_dim):
    j = pl.program_id(1)
    u = _rms_rows(h_ref[...], g_ref[...]).astype(_BF16)

    assert 2 * head_dim == LANES
    first = lax.broadcasted_iota(jnp.int32, (1, LANES), 1) < head_dim
    for idx, gain_ref in ((0, qg_ref), (1, kg_ref)):
        a = _dot_nt(u, wqkv_ref[idx * fox_w:(idx + 1) * fox_w, :])
        for c in range(fox_w // LANES):
            cs = slice(c * LANES, (c + 1) * LANES)
            ac = a[:, cs]
            sq = ac * ac
            lo = jnp.sum(jnp.where(first, sq, 0.0), axis=1, keepdims=True)
            hi = jnp.sum(jnp.where(first, 0.0, sq), axis=1, keepdims=True)
            ss = jnp.where(first, lo, hi)
            qkv_ref[:, idx * fox_w + cs.start:idx * fox_w + cs.stop] = (
                ac * lax.rsqrt(ss * (1.0 / head_dim) + EPS) * gain_ref[:, cs]).astype(_BF16)
    qkv_ref[:, 2 * fox_w:3 * fox_w] = _dot_nt(u, wqkv_ref[2 * fox_w:3 * fox_w, :]).astype(_BF16)

    @pl.when(j == 0)
    def _():
        tail_s[...] = jnp.zeros(tail_s.shape, _F32)

    n_taps = cw_ref.shape[0]
    hr = tail_s.shape[0]
    assert n_taps - 1 <= hr and conv_w % (2 * LANES) == 0
    sub = lax.broadcasted_iota(jnp.int32, (hr, LANES), 0)
    for cp in range(conv_w // (2 * LANES)):
        gate_b = _dot_nt(u, wrest_ref[conv_w + cp * 2 * LANES:conv_w + (cp + 1) * 2 * LANES, :])
        for half in range(2):
            cs = slice((2 * cp + half) * LANES, (2 * cp + half + 1) * LANES)
            w_pair = jnp.concatenate(
                [wrest_ref[cs, :], wrest_ref[2 * conv_w + cs.start:2 * conv_w + cs.stop, :]], axis=0)
            r = _dot_nt(u, w_pair)
            zc = r[:, :LANES] * r[:, LANES:]
            y = cw_ref[n_taps - 1:n_taps, cs] * zc
            head = zc[0:hr, :]
            y_head = cw_ref[n_taps - 1:n_taps, cs] * head
            prev = tail_s[:, cs]
            for d in range(1, n_taps):
                w_d = cw_ref[n_taps - 1 - d:n_taps - d, cs]
                y = y + w_d * pltpu.roll(zc, d, axis=0)
                y_head = y_head + w_d * jnp.where(sub < d, pltpu.roll(prev, d, axis=0),
                                                  pltpu.roll(head, d, axis=0))
            gb = gate_b[:, half * LANES:(half + 1) * LANES]
            cc_ref[0:hr, cs] = (gb[0:hr, :] * y_head).astype(_BF16)
            cc_ref[hr:tm, cs] = (gb[hr:tm, :] * y[hr:tm, :]).astype(_BF16)
            tail_s[:, cs] = zc[tm - hr:tm, :]

    o = 3 * conv_w
    step = conv_w
    for i in range(gate_w // step):
        a = _dot_nt(u, wrest_ref[o + i * step:o + (i + 1) * step, :])
        gate_ref[:, i * step:(i + 1) * step] = jax.nn.sigmoid(a).astype(_BF16)

    a = _dot_nt(u, wf_ref[...]) + bf_ref[...]
    lf_ref[...] = jnp.minimum(a, 0.0) - jnp.log1p(jnp.exp(-jnp.abs(a)))


def _inproj(h, gain, w_qkv, w_rest, w_f, bf_pad, qg, kg, conv_w_l, *, batch, seq, dims):
    fox_w, conv_wd, gate_w, head_dim = dims
    n_tok, d_model = h.shape
    tm = TOKEN_TILE
    nt = seq // tm
    row = lambda b, j: (b * nt + j, 0)
    const2 = lambda b, j: (0, 0)
    return pl.pallas_call(
        functools.partial(_inproj_kernel, tm=tm, fox_w=fox_w, conv_w=conv_wd, gate_w=gate_w,
                          head_dim=head_dim),
        grid=(batch, nt),
        in_specs=[
            pl.BlockSpec((tm, d_model), row),
            _resident((1, d_model), const2),
            _resident(w_qkv.shape, const2),
            _resident(w_rest.shape, const2),
            _resident(w_f.shape, const2),
            _resident((1, LANES), const2),
            _resident((1, fox_w), const2),
            _resident((1, fox_w), const2),
            _resident(conv_w_l.shape, const2),
        ],
        out_specs=[
            pl.BlockSpec((tm, 3 * fox_w), row),
            pl.BlockSpec((tm, LANES), row),
            pl.BlockSpec((tm, conv_wd), row),
            pl.BlockSpec((tm, gate_w), row),
        ],
        out_shape=[
            jax.ShapeDtypeStruct((n_tok, 3 * fox_w), _BF16),
            jax.ShapeDtypeStruct((n_tok, LANES), _F32),
            jax.ShapeDtypeStruct((n_tok, conv_wd), _BF16),
            jax.ShapeDtypeStruct((n_tok, gate_w), _BF16),
        ],
        scratch_shapes=[pltpu.VMEM((2 * SUBLANES, conv_wd), _F32)],
        compiler_params=_params("arbitrary", "arbitrary"),
        name="inproj",
    )(h, gain, w_qkv, w_rest, w_f, bf_pad, qg, kg, conv_w_l)


def _decay_kernel(lf_ref, tri_ref, sq_ref, sk_ref, constq_ref, constk_ref, eq_ref, ek_ref,
                  *, seq, blk, n_heads):
    tri = tri_ref[...]
    lane = lax.broadcasted_iota(jnp.int32, (1, LANES), 1)
    parts = []
    for i in range(lf_ref.shape[0] // blk):
        if i % (seq // blk) == 0:
            carry = jnp.zeros((1, LANES), _F32)
        hi, mid, lo = _split3(lf_ref[i * blk:(i + 1) * blk, :])
        t = _dot(tri, jnp.concatenate([lo, mid, hi], axis=1))
        c = (t[:, :LANES] + t[:, LANES:2 * LANES]) + t[:, 2 * LANES:] + carry
        carry = c[blk - 1:blk, :]
        hi, mid, lo = _split3(c * LOG2E)
        parts.append(jnp.where(lane < n_heads, hi, jnp.where(lane < 2 * n_heads, mid, lo)))
    parts = jnp.concatenate(parts, axis=0)
    eq_ref[...] = (_dot(parts, sq_ref[...]) + constq_ref[...]).astype(_BF16)
    ek_ref[...] = (_dot(parts, sk_ref[...]) + constk_ref[...]).astype(_BF16)


def _decay(logf, consts, *, batch, seq, n_heads):
    tri, sq = consts[0], consts[1]
    blk = tri.shape[0]
    aug_w = sq.shape[1]
    n_tok = logf.shape[0]
    const = lambda b: (0, 0)
    out = jax.ShapeDtypeStruct((n_tok, aug_w), _BF16)
    return pl.pallas_call(
        functools.partial(_decay_kernel, seq=seq, blk=blk, n_heads=n_heads),
        grid=(batch // DECAY_SEQS_PER_STEP,),
        in_specs=[pl.BlockSpec((DECAY_SEQS_PER_STEP * seq, LANES), lambda b: (b, 0))]
        + [_resident(c.shape, const) for c in consts],
        out_specs=[pl.BlockSpec((DECAY_SEQS_PER_STEP * seq, aug_w), lambda b: (b, 0))] * 2,
        out_shape=[out, out],
        compiler_params=_params("arbitrary"),
        name="decay",
    )(logf, *consts)


def _decay_constants(n_heads, head_dim):
    assert DECAY_PARTS * n_heads <= LANES and 2 * DECAY_PARTS <= head_dim
    r = np.arange(CUMSUM_BLOCK)
    tri = (r[:, None] >= r[None, :]).astype(np.float32)
    width = (n_heads // 2) * LANES
    sq = np.zeros((LANES, width), np.float32)
    sk = np.zeros_like(sq)
    constq = np.zeros((1, width), np.float32)
    constk = np.zeros_like(constq)
    for h in range(n_heads):
        base = (h // 2) * LANES + (head_dim if h % 2 == 0 else 0)
        for part in range(DECAY_PARTS):
            sq[part * n_heads + h, base + part] = 1.0
            sk[part * n_heads + h, base + DECAY_PARTS + part] = -1.0
            constq[0, base + DECAY_PARTS + part] = 1.0
            constk[0, base + part] = 1.0
    return (jnp.asarray(tri, _BF16), jnp.asarray(sq, _BF16), jnp.asarray(sk, _BF16),
            jnp.asarray(constq), jnp.asarray(constk))


def _fox_kernel(q_ref, eq_ref, k_ref, ek_ref, v_ref, o_ref, *, seq, tq, tk, head_dim):
    lane = lax.broadcasted_iota(jnp.int32, (1, 2 * head_dim), 1)
    first = lane < head_dim
    pair_w = 2 * head_dim
    ones_col = jnp.where(lax.broadcasted_iota(jnp.int32, (tk, LANES), 1) == 0, 1.0, 0.0).astype(_BF16)

    def update(m, acc, q_rows, k_rows, v_rows, row_offset):
        s = _dot_nt(q_rows, k_rows)
        if row_offset is not None:
            rows = lax.broadcasted_iota(jnp.int32, s.shape, 0)
            cols = lax.broadcasted_iota(jnp.int32, s.shape, 1)
            s = jnp.where(rows + row_offset >= cols, s, MASK_VALUE)
        m_new = jnp.maximum(m, jnp.max(s, axis=1, keepdims=True))
        alpha = jnp.exp2(m - m_new)
        p = jnp.exp2(s - m_new).astype(_BF16)
        return m_new, alpha * acc + _dot(p, v_rows)

    n_q = seq // tq
    for pr in range(q_ref.shape[1] // pair_w):
        pc = slice(pr * pair_w, (pr + 1) * pair_w)
        q_aug, state = [], []
        for qi in range(n_q):
            qs = slice(qi * tq, (qi + 1) * tq)
            q = q_ref[qs, pc]
            eq = eq_ref[qs, pc]
            q_aug.append((jnp.where(first, q, eq), jnp.where(first, eq, q)))
            state.append([(jnp.full((tq, 1), MASK_VALUE, _F32),
                           jnp.zeros((tq, pair_w + LANES), _F32)) for _ in range(2)])

        for j in range(n_q):
            ks = slice(j * tk, (j + 1) * tk)
            kc = k_ref[ks, pc]
            v_aug = jnp.concatenate([v_ref[ks, pc], ones_col], axis=1)
            ekc = ek_ref[ks, pc]
            k_aug = (jnp.where(first, kc, ekc), jnp.where(first, ekc, kc))
            for qi in range(j, n_q):
                for e in range(2):
                    m, acc = state[qi][e]
                    state[qi][e] = update(m, acc, q_aug[qi][e], k_aug[e], v_aug,
                                          0 if j == qi else None)

        for qi in range(n_q):
            acc0, acc1 = state[qi][0][1], state[qi][1][1]
            o_ref[qi * tq:(qi + 1) * tq, pc] = jnp.where(
                first, acc0[:, :pair_w] / acc0[:, pair_w:pair_w + 1],
                acc1[:, :pair_w] / acc1[:, pair_w:pair_w + 1]).astype(_BF16)


def _fox(qkv, eq, ek, *, batch, seq, n_heads, head_dim):
    n_tok = qkv.shape[0]
    fox_w = n_heads * head_dim
    n_pairs = n_heads // 2
    pair_w = 2 * head_dim
    tq, tk = ATTN_Q_TILE, ATTN_K_TILE
    gw = ATTN_PAIRS_PER_STEP * pair_w
    n_groups = n_pairs // ATTN_PAIRS_PER_STEP
    assert tq == tk and pair_w == LANES and n_pairs % ATTN_PAIRS_PER_STEP == 0
    return pl.pallas_call(
        functools.partial(_fox_kernel, seq=seq, tq=tq, tk=tk, head_dim=head_dim),
        grid=(batch, n_groups),
        in_specs=[
            pl.BlockSpec((seq, gw), lambda b, p: (b, p)),
            pl.BlockSpec((seq, gw), lambda b, p: (b, p)),
            pl.BlockSpec((seq, gw), lambda b, p: (b, n_groups + p)),
            pl.BlockSpec((seq, gw), lambda b, p: (b, p)),
            pl.BlockSpec((seq, gw), lambda b, p: (b, 2 * n_groups + p)),
        ],
        out_specs=pl.BlockSpec((seq, gw), lambda b, p: (b, p)),
        out_shape=jax.ShapeDtypeStruct((n_tok, fox_w), _BF16),
        compiler_params=_params("arbitrary", "arbitrary"),
        name="fox",
    )(qkv, eq, qkv, ek, qkv)


def _post_kernel(a_ref, cc_ref, gate_ref, h_ref, wua_ref, wub_ref, wo_ref, gq_ref, wcq_ref,
                 cqg_ref, km_ref, vm_ref, wco_ref, *rest, n_heads, head_dim, n_staged):
    staged_src, (o_ref, *staged_dst, merged_s) = rest[:n_staged], rest[n_staged:]
    _cast_staged(staged_src, staged_dst)
    d_model = h_ref.shape[-1]
    scale = head_dim ** -0.5
    a = a_ref[...]
    cc = cc_ref[...]
    for c in range(d_model // MERGE_CHUNK):
        cs = slice(c * MERGE_CHUNK, (c + 1) * MERGE_CHUNK)
        up_a = _dot(a, wua_ref[:, cs])
        up_b = _dot(cc, wub_ref[:, cs])
        merged_s[:, cs] = (gate_ref[:, cs].astype(_F32) * up_a
                           + gate_ref[:, d_model + cs.start:d_model + cs.stop].astype(_F32) * up_b
                           ).astype(_BF16)
    h1 = h_ref[...] + _dot(merged_s[...], wo_ref[...])

    u = _rms_rows(h1, gq_ref[...]).astype(_BF16)
    qm = _dot(u, wcq_ref[...])
    heads = []
    for e in range(n_heads):
        sl = slice(e * head_dim, (e + 1) * head_dim)
        qe = qm[:, sl]
        ms = jnp.mean(qe * qe, axis=-1, keepdims=True)
        qn = (qe * lax.rsqrt(ms + EPS) * cqg_ref[...]).astype(_BF16)
        s = _dot_nt(qn, km_ref[:, sl]) * scale
        p = jnp.exp(s - jnp.max(s, axis=1, keepdims=True))
        den = jnp.sum(p, axis=1, keepdims=True)
        heads.append((_dot(p.astype(_BF16), vm_ref[:, sl]) / den).astype(_BF16))
    o_ref[...] = h1 + _dot(jnp.concatenate(heads, axis=1), wco_ref[...])


def _post(a_out, cc, gates, h, layer, w_up_a, w_up_b, w_o, gq, w_cq, cq_gain_l, kmem, vmem, w_co,
          staged, *, batch, seq):
    n_tok, d_model = h.shape
    tm = TOKEN_TILE
    nt = seq // tm
    fox_w = a_out.shape[1]
    conv_wd = cc.shape[1]
    mem_w = kmem.shape[-1]
    m_tok = kmem.shape[1] // batch
    head_dim = cq_gain_l.shape[-1]
    row = lambda b, j: (b * nt + j, 0)
    const2 = lambda b, j: (0, 0)
    st_in, st_out, st_shapes = _cast_staging(staged, batch * nt, lambda b, j: b * nt + j)
    return pl.pallas_call(
        functools.partial(_post_kernel, n_heads=mem_w // head_dim, head_dim=head_dim,
                          n_staged=len(staged)),
        grid=(batch, nt),
        in_specs=[
            pl.BlockSpec((tm, fox_w), row),
            pl.BlockSpec((tm, conv_wd), row),
            pl.BlockSpec((tm, 2 * d_model), row),
            pl.BlockSpec((tm, d_model), row),
            _resident(w_up_a.shape, const2),
            _resident(w_up_b.shape, const2),
            _resident(w_o.shape, const2),
            _resident((1, d_model), const2),
            _resident(w_cq.shape, const2),
            _resident((1, head_dim), const2),
            pl.BlockSpec((None, m_tok, mem_w), lambda b, j: (layer, b, 0)),
            pl.BlockSpec((None, m_tok, mem_w), lambda b, j: (layer, b, 0)),
            _resident(w_co.shape, const2),
        ] + st_in,
        out_specs=[pl.BlockSpec((tm, d_model), row)] + st_out,
        out_shape=[jax.ShapeDtypeStruct((n_tok, d_model), _F32)] + st_shapes,
        scratch_shapes=[pltpu.VMEM((tm, d_model), _BF16)],
        compiler_params=_params("arbitrary", "arbitrary"),
        name="post",
    )(a_out, cc, gates, h, w_up_a, w_up_b, w_o, gq, w_cq, cq_gain_l, kmem, vmem, w_co,
      *[s[0] for s in staged])


def _ffn_kernel(h_ref, g_ref, wgu_ref, wd_ref, *rest, d_ff, chunk, n_staged):
    staged_src, (o_ref, *staged_dst, act_s) = rest[:n_staged], rest[n_staged:]
    _cast_staged(staged_src, staged_dst)
    h = h_ref[...]
    u = _rms_rows(h, g_ref[...]).astype(_BF16)
    for c in range(d_ff // chunk):
        gate = _dot(u, wgu_ref[:, c * chunk:(c + 1) * chunk])
        up = _dot(u, wgu_ref[:, d_ff + c * chunk:d_ff + (c + 1) * chunk])
        act_s[:, c * chunk:(c + 1) * chunk] = (gate * jax.nn.sigmoid(gate) * up).astype(_BF16)
    o_ref[...] = h + _dot(act_s[...], wd_ref[...])


def _ffn(h, gain, w_gu, w_down, staged):
    n_tok, d_model = h.shape
    d_ff = w_down.shape[0]
    tm = TOKEN_TILE
    assert d_ff % FFN_CHUNK == 0
    st_in, st_out, st_shapes = _cast_staging(staged, n_tok // tm, lambda i: i)
    return pl.pallas_call(
        functools.partial(_ffn_kernel, d_ff=d_ff, chunk=FFN_CHUNK, n_staged=len(staged)),
        grid=(n_tok // tm,),
        in_specs=[
            pl.BlockSpec((tm, d_model), lambda i: (i, 0)),
            _resident((1, d_model), lambda i: (0, 0)),
            _resident(w_gu.shape, lambda i: (0, 0)),
            _resident(w_down.shape, lambda i: (0, 0)),
        ] + st_in,
        out_specs=[pl.BlockSpec((tm, d_model), lambda i: (i, 0))] + st_out,
        out_shape=[jax.ShapeDtypeStruct((n_tok, d_model), _F32)] + st_shapes,
        scratch_shapes=[pltpu.VMEM((tm, d_ff), _BF16)],
        compiler_params=_params("arbitrary"),
        name="ffn",
    )(h, gain, w_gu, w_down, *[s[0] for s in staged])


def kernel(x, mem, norm_mix, w_in, b_f, q_gain, k_gain, conv_w, w_up_a, w_up_b, w_o,
           norm_mem_q, norm_mem_kv, w_cq, w_ckv, cq_gain, ck_gain, w_co, norm_ffn, w_gu, w_down):
    batch, seq, d_model = x.shape
    n_layers = w_in.shape[0]
    head_dim = q_gain.shape[-1]
    n_heads = b_f.shape[-1]
    fox_w = n_heads * head_dim
    conv_wd = conv_w.shape[-1]
    gate_w = 2 * d_model
    assert seq % TOKEN_TILE == 0 and seq % ATTN_Q_TILE == 0 and seq % CUMSUM_BLOCK == 0
    assert w_in.shape[-1] == 3 * fox_w + n_heads + 3 * conv_wd + gate_w

    f0 = 3 * fox_w
    w_in_t = jnp.swapaxes(w_in, 1, 2)
    rest_rows = w_in_t.shape[1] - f0 - n_heads
    f_pad = LANES - DECAY_PARTS * n_heads
    w_f = jnp.pad(jnp.tile(w_in_t[:, f0:f0 + n_heads], (1, DECAY_PARTS, 1)),
                  ((0, 0), (0, f_pad), (0, 0))).astype(_BF16)
    bf_pad = jnp.pad(jnp.tile(b_f, (1, DECAY_PARTS)), ((0, 0), (0, f_pad)))

    def mixer_sources(layer):
        return [(w_in_t, layer, 0, f0), (w_in_t, layer, f0 + n_heads, rest_rows)] + [
            (w, layer, 0, w.shape[1]) for w in (w_up_a, w_up_b, w_o, w_cq, w_co)]

    qg = jnp.tile(q_gain, (1, n_heads)) * (head_dim ** -0.5 * LOG2E)
    kg = jnp.tile(k_gain, (1, n_heads))

    decay_consts = _decay_constants(n_heads, head_dim)

    kmem, vmem, *mixer_w = _memkv(mem, norm_mem_kv, w_ckv, ck_gain, mixer_sources(0))

    h = x.reshape(batch * seq, d_model)
    dims = (fox_w, conv_wd, gate_w, head_dim)
    for l in range(n_layers):
        w_qkv, w_rest, w_up_a_bf, w_up_b_bf, w_o_bf, w_cq_bf, w_co_bf = mixer_w
        qkv, logf, cc, gates = _inproj(h, norm_mix[l][None], w_qkv, w_rest, w_f[l], bf_pad[l][None],
                                       qg[l][None], kg[l][None], conv_w[l],
                                       batch=batch, seq=seq, dims=dims)
        eq, ek = _decay(logf, decay_consts, batch=batch, seq=seq, n_heads=n_heads)
        a_out = _fox(qkv, eq, ek, batch=batch, seq=seq, n_heads=n_heads, head_dim=head_dim)
        ffn_sources = [(w_gu, l, 0, w_gu.shape[1]), (w_down, l, 0, w_down.shape[1])]
        h, w_gu_bf, w_down_bf = _post(
            a_out, cc, gates, h, l, w_up_a_bf, w_up_b_bf, w_o_bf, norm_mem_q[l][None], w_cq_bf,
            cq_gain[l][None], kmem, vmem, w_co_bf, ffn_sources, batch=batch, seq=seq)
        next_sources = mixer_sources(l + 1) if l + 1 < n_layers else []
        h, *mixer_w = _ffn(h, norm_ffn[l][None], w_gu_bf, w_down_bf, next_sources)
    return h.reshape(batch, seq, d_model)
```
